```python
import math
import jax, jax.numpy as jnp
from jax import lax
import numpy as np

D_MODEL = 2048
BATCH = 16
SEQ = 256
DEPTH = 4
DEC_BATCH = 4
DEC_SEQ = 4096
PAST_LEN = 256

GRID_W = 64
H_DIFF = 8
DQK = 64
DV_DIFF = 128
W_DIFF = H_DIFF * DV_DIFF
H_RET = 8
DK_RET = 128
DV_RET = 128
W_RET = H_RET * DV_RET
MIX_W = W_DIFF + W_RET
N_IN = 3 * H_DIFF * 2 * DQK + 4 * W_RET
SPLITS = [1024, 2048, 3072, 4096, 5120, 6144]
ROPE_AXIS = DQK // 2
ROPE_BASE = 10000.0
Q_BLOCK = 128
RET_CHUNK = 128
N_EXPERTS = 16
N_GROUPS = 4
EXPERTS_PER_GROUP = N_EXPERTS // N_GROUPS
TOP_K = 2
D_FF = 1408
MOE_BLOCK = 128
LN_EPS = 1e-5
DEEPNORM_ALPHA = (2 * DEPTH) ** 0.25
DEEPNORM_BETA = (8 * DEPTH) ** -0.25

kernel_name = 'hybrid_diffattn_retention_grouped_moe_diffusion_step'

F32 = jnp.float32


def _layer_norm(x, g, b):
    xf = x.astype(F32)
    mu = jnp.mean(xf, -1, keepdims=True)
    xc = xf - mu
    var = jnp.mean(xc * xc, -1, keepdims=True)
    return (xc * lax.rsqrt(var + LN_EPS) * g.astype(F32) + b.astype(F32)).astype(x.dtype)


def _head_norm(x):
    xf = x.astype(F32)
    mu = jnp.mean(xf, -1, keepdims=True)
    xc = xf - mu
    return xc * lax.rsqrt(jnp.mean(xc * xc, -1, keepdims=True) + LN_EPS)


def _rms_norm(x, g):
    xf = x.astype(F32)
    return xf * lax.rsqrt(jnp.mean(xf * xf, -1, keepdims=True) + LN_EPS) * g.astype(F32)


def _rope_tables(n_tok):
    rows = n_tok // GRID_W
    r = jnp.repeat(jnp.arange(rows, dtype=F32), GRID_W)
    col = jnp.tile(jnp.arange(GRID_W, dtype=F32), rows)
    inv = ROPE_BASE ** (-jnp.arange(0, ROPE_AXIS, 2, dtype=F32) / ROPE_AXIS)
    ar = r[:, None] * inv[None]
    ac = col[:, None] * inv[None]
    ang = jnp.concatenate([ar, ar, ac, ac], -1)
    return jnp.cos(ang), jnp.sin(ang)


def _apply_rope(x, cos, sin):
    h = ROPE_AXIS // 2
    xr, xc = x[..., :ROPE_AXIS], x[..., ROPE_AXIS:]
    rot = jnp.concatenate([-xr[..., h:], xr[..., :h], -xc[..., h:], xc[..., :h]], -1)
    c = cos[None, :, None, None, :]
    s = sin[None, :, None, None, :]
    return x * c + rot * s


def _diff_attention(q, k, v, lam):
    b_, h_, n, _, _ = q.shape
    nb = n // Q_BLOCK
    qb = jnp.moveaxis(q.reshape(b_, h_, nb, Q_BLOCK, 2, DQK), 2, 0)
    vf = v.astype(F32)
    scale = DQK ** -0.5

    def block(qi):
        s = jnp.einsum('bhqcd,bhkcd->bhcqk', qi, k).astype(F32) * scale
        p = jax.nn.softmax(s, axis=-1)
        w = p[:, :, 0] - lam * p[:, :, 1]
        return jnp.einsum('bhqk,bhkv->bhqv', w, vf)

    o = lax.map(block, qb)
    return jnp.moveaxis(o, 0, 2).reshape(b_, h_, n, DV_DIFF)


def _retention_chunkwise(q, k, v, gamma, s0):
    q = q.astype(F32)
    k = k.astype(F32)
    v = v.astype(F32)
    b_, h_, t, _ = q.shape
    nc = t // RET_CHUNK

    def chunks(a):
        return jnp.moveaxis(a.reshape(b_, h_, nc, RET_CHUNK, a.shape[-1]), 2, 0)

    log_g = jnp.log(gamma.astype(F32))
    pos = jnp.arange(RET_CHUNK, dtype=F32)
    dist = pos[:, None] - pos[None, :]
    intra = jnp.where(dist >= 0, jnp.exp(log_g[:, None, None] * jnp.maximum(dist, 0.0)), 0.0)
    q_decay = jnp.exp(log_g[:, None] * (pos + 1.0))[..., None]
    k_decay = jnp.exp(log_g[:, None] * (RET_CHUNK - 1.0 - pos))[..., None]
    chunk_decay = jnp.exp(log_g * RET_CHUNK)[:, None, None]

    def step(s, qkv):
        qi, ki, vi = qkv
        a = jnp.einsum('bhqd,bhkd->bhqk', qi, ki) * intra
        o = jnp.einsum('bhqk,bhkv->bhqv', a, vi) + jnp.einsum('bhqd,bhdv->bhqv', qi * q_decay, s)
        s = chunk_decay * s + jnp.einsum('bhkd,bhkv->bhdv', ki * k_decay, vi)
        return s, o

    s, o = lax.scan(step, s0.astype(F32), (chunks(q), chunks(k), chunks(v)))
    return jnp.moveaxis(o, 0, 2).reshape(b_, h_, t, v.shape[-1]), s


def _ret_gammas():
    h = jnp.arange(H_RET, dtype=F32)
    return 1.0 - jnp.exp2(-5.0 - h), 1.0 - jnp.exp2(-5.5 - h)


def _token_mixer(h, w_in_l, w_out_l, lam, lam_init, subln_g_l, rope, ctx_k, ctx_v, s0_f, s0_b):
    b_, n, _ = h.shape
    z = jnp.dot(h, w_in_l)
    qd, kd, vd, qr, kr, vr, gr = jnp.split(z, SPLITS, axis=-1)
    q = qd.reshape(b_, n, H_DIFF, 2, DQK)
    k = kd.reshape(b_, n, H_DIFF, 2, DQK)
    if rope is not None:
        q = _apply_rope(q, rope[0], rope[1])
        k = _apply_rope(k, rope[0], rope[1])
    q = q.transpose(0, 2, 1, 3, 4)
    k = k.transpose(0, 2, 1, 3, 4)
    v = vd.reshape(b_, n, H_DIFF, DV_DIFF).transpose(0, 2, 1, 3)
    if ctx_k is None:
        keys, vals = k, v
    else:
        lc = ctx_k.shape[2]
        keys = jnp.concatenate([ctx_k.reshape(b_, H_DIFF, lc, 2, DQK).astype(k.dtype), k], axis=2)
        vals = jnp.concatenate([ctx_v.astype(v.dtype), v], axis=2)
    o_d = _diff_attention(q, keys, vals, lam)
    o_d = _rms_norm(o_d, subln_g_l) * (1.0 - lam_init)
    o_d = o_d.transpose(0, 2, 1, 3).reshape(b_, n, W_DIFF)
    gam_f, gam_b = _ret_gammas()
    qh = qr.reshape(b_, n, H_RET, DK_RET).transpose(0, 2, 1, 3)
    kh = kr.reshape(b_, n, H_RET, DK_RET).transpose(0, 2, 1, 3) * (DK_RET ** -0.5)
    vh = vr.reshape(b_, n, H_RET, DV_RET).transpose(0, 2, 1, 3)
    o_f, s_f = _retention_chunkwise(qh, kh, vh, gam_f, s0_f)
    o_b, s_b = _retention_chunkwise(jnp.flip(qh, 2), jnp.flip(kh, 2), jnp.flip(vh, 2), gam_b, s0_b)
    o_r = _head_norm(o_f + jnp.flip(o_b, 2))
    o_r = o_r.transpose(0, 2, 1, 3).reshape(b_, n, W_RET) * jax.nn.silu(gr.astype(F32))
    out = jnp.dot(jnp.concatenate([o_d, o_r], -1).astype(h.dtype), w_out_l)
    own_k = k.reshape(b_, H_DIFF, n, 2 * DQK)
    return out, own_k, v, s_f, s_b


def _moe(h, w_router, b_router, w1_l, w3_l, w2_l):
    b_, n, d = h.shape
    t = b_ * n
    x2 = h.reshape(t, d)
    logits = jnp.dot(x2, w_router).astype(F32) + b_router.astype(F32)
    probs = jax.nn.softmax(logits, -1)
    grp = probs.reshape(t, N_GROUPS, EXPERTS_PER_GROUP)
    grp_score = lax.top_k(grp, TOP_K)[0].sum(-1)
    g_sel = jnp.argmax(grp_score, -1)
    keep = (g_sel[:, None] == jnp.arange(N_GROUPS)[None])[:, :, None]
    masked = jnp.where(keep, grp, -1.0).reshape(t, N_EXPERTS)
    gate, idx = lax.top_k(masked, TOP_K)
    gate = gate / gate.sum(-1, keepdims=True)
    a = t * TOP_K
    flat_e = idx.reshape(a)
    order = jnp.argsort(flat_e)
    sorted_e = flat_e[order]
    counts = jnp.bincount(flat_e, length=N_EXPERTS)
    padded = (counts + MOE_BLOCK - 1) // MOE_BLOCK * MOE_BLOCK
    ends_p = jnp.cumsum(padded)
    starts_p = ends_p - padded
    starts = jnp.cumsum(counts) - counts
    dest = starts_p[sorted_e] + jnp.arange(a) - starts[sorted_e]
    n_blocks = -(-a // MOE_BLOCK) + N_EXPERTS
    p_len = n_blocks * MOE_BLOCK
    slot_tok = jnp.full((p_len,), t, jnp.int32).at[dest].set((order // TOP_K).astype(jnp.int32))
    slot_gate = jnp.zeros((p_len,), F32).at[dest].set(gate.reshape(a)[order])
    block_e = jnp.clip(jnp.searchsorted(ends_p, jnp.arange(n_blocks) * MOE_BLOCK, side='right'), 0, N_EXPERTS - 1)
    x_pad = jnp.concatenate([x2, jnp.zeros((1, d), x2.dtype)], 0)
    xb = x_pad[slot_tok].reshape(n_blocks, MOE_BLOCK, d)

    def expert_block(args):
        xi, e = args
        return jnp.dot(jax.nn.silu(jnp.dot(xi, w1_l[e])) * jnp.dot(xi, w3_l[e]), w2_l[e])

    yb = lax.map(expert_block, (xb, block_e))
    y = jnp.zeros((t + 1, d), F32).at[slot_tok].add(yb.reshape(p_len, d).astype(F32) * slot_gate[:, None])
    return y[:t].reshape(b_, n, d).astype(h.dtype)


def _trunk_layer(x, mod, rope, ctx_k, ctx_v, s0_f, s0_b, w_in_l, w_out_l, lam, lam_init, subln_g_l,
                 ln_g_l, ln_b_l, w_router, b_router, w1_l, w3_l, w2_l):
    sh_a, sc_a, g_a, sh_f, sc_f, g_f = jnp.split(mod, 6, axis=-1)
    h = x * (1.0 + sc_a) + sh_a
    mix, own_k, own_v, s_f, s_b = _token_mixer(h, w_in_l, w_out_l, lam, lam_init, subln_g_l, rope,
                                               ctx_k, ctx_v, s0_f, s0_b)
    x = _layer_norm(DEEPNORM_ALPHA * x + g_a * mix, ln_g_l[0], ln_b_l[0])
    h = x * (1.0 + sc_f) + sh_f
    x = _layer_norm(DEEPNORM_ALPHA * x + g_f * _moe(h, w_router, b_router, w1_l, w3_l, w2_l),
                    ln_g_l[1], ln_b_l[1])
    return x, own_k, own_v, s_f, s_b


def _normal(k, shape, scale):
    return jax.random.normal(k, shape, F32) * scale


def setup_inputs(seed: int = 0) -> dict:
    key = jax.random.key(seed)
    ks = jax.random.split(key, 24)
    d = D_MODEL
    return {
        'x_prompt': _normal(ks[0], (BATCH, SEQ, d), 1.0),
        'x_sample': _normal(ks[1], (DEC_BATCH, DEC_SEQ, d), 1.0),
        'c': _normal(ks[2], (DEC_BATCH, d), 1.0),
        'cache_k': _normal(ks[3], (DEC_BATCH, DEPTH, H_DIFF, PAST_LEN, 2 * DQK), 1.0),
        'cache_v': _normal(ks[4], (DEC_BATCH, DEPTH, H_DIFF, PAST_LEN, DV_DIFF), 1.0),
        'state_ret': _normal(ks[5], (DEC_BATCH, DEPTH, 2, H_RET, DK_RET, DV_RET), 1.0),
        'c_ctx': _normal(ks[6], (d,), 1.0),
        'w_mod': _normal(ks[7], (DEPTH, d, 6 * d), 0.5 * d ** -0.5),
        'b_mod': _normal(ks[8], (DEPTH, 6 * d), 0.02),
        'w_in': _normal(ks[9], (DEPTH, d, N_IN), d ** -0.5),
        'w_out': _normal(ks[10], (DEPTH, MIX_W, d), DEEPNORM_BETA * MIX_W ** -0.5),
        'lam_q1': _normal(ks[11], (DEPTH, DQK), 0.1),
        'lam_k1': _normal(ks[12], (DEPTH, DQK), 0.1),
        'lam_q2': _normal(ks[13], (DEPTH, DQK), 0.1),
        'lam_k2': _normal(ks[14], (DEPTH, DQK), 0.1),
        'subln_g': 1.0 + _normal(ks[15], (DEPTH, DV_DIFF), 0.02),
        'ln_g': 1.0 + _normal(ks[16], (DEPTH, 2, d), 0.02),
        'ln_b': _normal(ks[17], (DEPTH, 2, d), 0.02),
        'w_router': _normal(ks[18], (d, N_EXPERTS), d ** -0.5),
        'b_router': _normal(ks[19], (N_EXPERTS,), 0.01),
        'w1': _normal(ks[20], (DEPTH, N_EXPERTS, d, D_FF), d ** -0.5),
        'w3': _normal(ks[21], (DEPTH, N_EXPERTS, d, D_FF), d ** -0.5),
        'w2': _normal(ks[22], (DEPTH, N_EXPERTS, D_FF, d), DEEPNORM_BETA * D_FF ** -0.5),
    }


def reference(x_prompt, x_sample, c, cache_k, cache_v, state_ret, c_ctx, w_mod, b_mod, w_in, w_out,
              lam_q1, lam_k1, lam_q2, lam_k2, subln_g, ln_g, ln_b, w_router, b_router, w1, w3, w2):
    xp = x_prompt
    bp = x_prompt.shape[0]
    zero_s = jnp.zeros((bp, H_RET, DK_RET, DV_RET), F32)
    ks_out, vs_out, ss_out = [], [], []
    lams = []
    for layer in range(DEPTH):
        lam_init = 0.8 - 0.6 * math.exp(-0.3 * layer)
        lam = (jnp.exp(jnp.sum(lam_q1[layer].astype(F32) * lam_k1[layer].astype(F32)))
               - jnp.exp(jnp.sum(lam_q2[layer].astype(F32) * lam_k2[layer].astype(F32))) + lam_init)
        lams.append((lam, lam_init))
        mod = jnp.dot(jax.nn.silu(c_ctx), w_mod[layer]) + b_mod[layer]
        xp, own_k, own_v, s_f, s_b = _trunk_layer(
            xp, mod, None, None, None, zero_s, zero_s, w_in[layer], w_out[layer], lam, lam_init,
            subln_g[layer], ln_g[layer], ln_b[layer], w_router, b_router, w1[layer], w3[layer], w2[layer])
        ks_out.append(own_k)
        vs_out.append(own_v)
        ss_out.append(jnp.stack([s_f, s_b], axis=1))
    new_cache_k = jnp.stack(ks_out, axis=1)
    new_cache_v = jnp.stack(vs_out, axis=1)
    new_state_ret = jnp.stack(ss_out, axis=1)
    xs = x_sample
    rope = _rope_tables(x_sample.shape[1])
    for layer in range(DEPTH):
        lam, lam_init = lams[layer]
        mod = (jnp.dot(jax.nn.silu(c), w_mod[layer]) + b_mod[layer])[:, None, :]
        xs, _, _, _, _ = _trunk_layer(
            xs, mod, rope, cache_k[:, layer], cache_v[:, layer], state_ret[:, layer, 0], state_ret[:, layer, 1],
            w_in[layer], w_out[layer], lam, lam_init, subln_g[layer], ln_g[layer], ln_b[layer],
            w_router, b_router, w1[layer], w3[layer], w2[layer])
    return (xp, xs, new_cache_k, new_cache_v, new_state_ret)
```

```python
import functools
import math

import jax
import jax.numpy as jnp
from jax import lax
from jax.experimental import pallas as pl
from jax.experimental.pallas import tpu as pltpu

F32 = jnp.float32
BF16 = jnp.bfloat16

GRID_W = 64
ROPE_BASE = 10000.0
RET_CHUNK = 128
N_GROUPS = 4
TOP_K = 2
LN_EPS = 1e-5
HEAD_W = 128
MOE_ROWS = 256
VMEM_LIMIT = 56 * 1024 * 1024

_NT = (((1,), (1,)), ((), ()))


def _tile(n, pref, mult=8):
    t = min(n, pref)
    while n % t or (t % mult and t != n):
        t -= 1
    return t


def _params(sem, vmem=VMEM_LIMIT):
    return pltpu.CompilerParams(dimension_semantics=sem, vmem_limit_bytes=vmem)


def _mod_kernel(c_ref, w_ref, b_ref, o_ref):
    c = c_ref[...]
    s = c * jax.nn.sigmoid(c)
    o_ref[...] = jnp.dot(s, w_ref[...], precision=lax.Precision.HIGHEST,
                         preferred_element_type=F32) + b_ref[...]


def _modulation(cs, w_mod, b_mod):
    depth, d, n = w_mod.shape
    rows = cs.shape[0]
    tn = _tile(n, 1024, 128)
    return pl.pallas_call(
        _mod_kernel,
        grid=(depth, n // tn),
        in_specs=[pl.BlockSpec((rows, d), lambda l, j: (0, 0)),
                  pl.BlockSpec((None, d, tn), lambda l, j: (l, 0, j)),
                  pl.BlockSpec((None, 1, tn), lambda l, j: (l, 0, j))],
        out_specs=pl.BlockSpec((None, rows, tn), lambda l, j: (l, 0, j)),
        out_shape=jax.ShapeDtypeStruct((depth, rows, n), F32),
        compiler_params=_params(("parallel", "parallel")),
        name="adaln_mod",
    )(cs, w_mod, b_mod.reshape(depth, 1, n))


class _Tokens:
    def __init__(self, batch, seq, dec_batch, dec_seq, n_cond):
        self.t_p = batch * seq
        self.t_s = dec_batch * dec_seq
        self.t = self.t_p + self.t_s
        self.dec_seq = dec_seq
        self.n_cond = n_cond

    def tile(self, pref):
        return _tile(math.gcd(self.t_p, self.dec_seq), pref)

    def mod_spec(self, layer, which, tm, d):
        t_p, dec_seq, n_cond = self.t_p, self.dec_seq, self.n_cond

        def idx(i, *_):
            r = i * tm
            cond = jnp.where(r < t_p, 0, 1 + (r - t_p) // dec_seq)
            return ((layer * n_cond + cond) * 6 + which, 0, 0)

        return pl.BlockSpec((None, 1, d), idx)


def _vec_spec(d):
    return pl.BlockSpec((1, d), lambda i, *_: (0, 0))


def _premod_kernel(x_ref, sh_ref, sc_ref, h_ref):
    h_ref[...] = (x_ref[...] * (1.0 + sc_ref[...]) + sh_ref[...]).astype(BF16)


def _premod(x, mod6, tok):
    t, d = x.shape
    tm = tok.tile(512)
    return pl.pallas_call(
        _premod_kernel,
        grid=(t // tm,),
        in_specs=[pl.BlockSpec((tm, d), lambda i: (i, 0)),
                  tok.mod_spec(0, 0, tm, d), tok.mod_spec(0, 1, tm, d)],
        out_specs=pl.BlockSpec((tm, d), lambda i: (i, 0)),
        out_shape=jax.ShapeDtypeStruct((t, d), BF16),
        compiler_params=_params(("parallel",)),
        name="premod",
    )(x, mod6, mod6)


def _matmul_kernel(a_ref, w_ref, o_ref):
    o_ref[...] = jnp.dot(a_ref[...], w_ref[...], preferred_element_type=F32)


def _in_proj(h, w_in_b, layer):
    t, d = h.shape
    n = w_in_b.shape[-1]
    tm = _tile(t, 1024)
    tn = _tile(n, 1024, 128)
    return pl.pallas_call(
        _matmul_kernel,
        grid=(t // tm, n // tn),
        in_specs=[pl.BlockSpec((tm, d), lambda i, j: (i, 0)),
                  pl.BlockSpec((None, d, tn), lambda i, j: (layer, 0, j))],
        out_specs=pl.BlockSpec((tm, tn), lambda i, j: (i, j)),
        out_shape=jax.ShapeDtypeStruct((t, n), F32),
        compiler_params=_params(("parallel", "parallel")),
        name="in_proj",
    )(h, w_in_b)


def _prep_kernel(z_ref, cos_ref, sa_ref, sb_ref, o_ref, *, n_rot, n_q):
    cos, sa, sb = cos_ref[...], sa_ref[...], sb_ref[...]
    for j in range(n_rot):
        x = z_ref[:, j * HEAD_W:(j + 1) * HEAD_W]
        r = x * cos + pltpu.roll(x, HEAD_W - 16, 1) * sa + pltpu.roll(x, 16, 1) * sb
        if j < n_q:
            r = r * 0.125
        o_ref[:, j * HEAD_W:(j + 1) * HEAD_W] = r.astype(BF16)
    o_ref[:, n_rot * HEAD_W:] = z_ref[:, n_rot * HEAD_W:].astype(BF16)


def _rope_prep(z, tables, tok, h_diff):
    cos, sa, sb = tables
    w = 3 * h_diff * HEAD_W
    tm = _tile(tok.dec_seq, 256)
    off = tok.t_p // tm
    nseq = tok.dec_seq // tm
    tab = pl.BlockSpec((tm, HEAD_W), lambda i: (i % nseq, 0))
    return pl.pallas_call(
        functools.partial(_prep_kernel, n_rot=2 * h_diff, n_q=h_diff),
        grid=(tok.t_s // tm,),
        in_specs=[pl.BlockSpec((tm, w), lambda i: (off + i, 0)), tab, tab, tab],
        out_specs=pl.BlockSpec((tm, w), lambda i: (i, 0)),
        out_shape=jax.ShapeDtypeStruct((tok.t_s, w), BF16),
        compiler_params=_params(("parallel",)),
        name="rope_prep",
    )(z, cos, sa, sb)


def _lam_value(lamp_ref, lam_init):
    lp = lamp_ref[...]
    a = jnp.sum(lp[0:1] * lp[1:2], axis=-1, keepdims=True)
    b = jnp.sum(lp[2:3] * lp[3:4], axis=-1, keepdims=True)
    return jnp.exp(a) - jnp.exp(b) + lam_init


def _softmax_pv(qc, chunks):
    m = l = acc = None
    for kk, vv in chunks:
        s = lax.dot_general(qc, kk, _NT, preferred_element_type=F32)
        cm = jnp.max(s, axis=-1, keepdims=True)
        if m is None:
            m_new = cm
        else:
            m_new = jnp.maximum(m, cm)
        p = jnp.exp(s - m_new)
        ps = jnp.sum(p, axis=-1, keepdims=True)
        pv = jnp.dot(p.astype(BF16), vv, preferred_element_type=F32)
        if m is None:
            l, acc = ps, pv
        else:
            alpha = jnp.exp(m - m_new)
            l = alpha * l + ps
            acc = alpha * acc + pv
        m = m_new
    return acc / l


def _diff_out(q, chunks, lam, g, lam_init):
    lane = lax.broadcasted_iota(jnp.int32, q.shape, 1)
    zero = jnp.zeros_like(q)
    o1 = _softmax_pv(jnp.where(lane < HEAD_W // 2, q, zero), chunks)
    o2 = _softmax_pv(jnp.where(lane >= HEAD_W // 2, q, zero), chunks)
    o = o1 - lam * o2
    o = o * lax.rsqrt(jnp.mean(o * o, axis=-1, keepdims=True) + LN_EPS) * g
    return o * (1.0 - lam_init)


def _attn_prompt_kernel(lamp_ref, q_ref, k_ref, v_ref, g_ref, o_ref, ok_ref, ov_ref, *, lam_init):
    k = k_ref[...]
    v = v_ref[...]
    ok_ref[...] = k
    ov_ref[...] = v
    q = (q_ref[...] * 0.125).astype(BF16)
    lam = _lam_value(lamp_ref, lam_init)
    o = _diff_out(q, [(k.astype(BF16), v.astype(BF16))], lam, g_ref[...], lam_init)
    o_ref[...] = o.astype(BF16)


def _attn_prompt(z, lamp, subln_g, layer, lam_init, batch, seq, h_diff):
    t_p = batch * seq
    blk = lambda c0: pl.BlockSpec((seq, HEAD_W), lambda b, h: (b, c0 + h))
    own = pl.BlockSpec((None, None, seq, HEAD_W), lambda b, h: (b, h, 0, 0))
    return pl.pallas_call(
        functools.partial(_attn_prompt_kernel, lam_init=lam_init),
        grid=(batch, h_diff),
        in_specs=[pl.BlockSpec((None, 4, HEAD_W // 2), lambda b, h: (layer, 0, 0)),
                  blk(0), blk(h_diff), blk(2 * h_diff),
                  pl.BlockSpec((None, 1, HEAD_W), lambda b, h: (layer, 0, 0))],
        out_specs=[pl.BlockSpec((seq, HEAD_W), lambda b, h: (b, h)), own, own],
        out_shape=[jax.ShapeDtypeStruct((t_p, h_diff * HEAD_W), BF16),
                   jax.ShapeDtypeStruct((batch, h_diff, seq, HEAD_W), F32),
                   jax.ShapeDtypeStruct((batch, h_diff, seq, HEAD_W), F32)],
        compiler_params=_params(("parallel", "parallel")),
        name="diff_attn_prompt",
    )(lamp, z, z, z, subln_g)


def _attn_sample_kernel(lamp_ref, q_ref, k_ref, v_ref, ck_ref, cv_ref, g_ref, o_ref, *, lam_init, kc):
    n = k_ref.shape[0]
    chunks = [(ck_ref[...], cv_ref[...])]
    for c in range(n // kc):
        chunks.append((k_ref[c * kc:(c + 1) * kc, :], v_ref[c * kc:(c + 1) * kc, :]))
    lam = _lam_value(lamp_ref, lam_init)
    o = _diff_out(q_ref[...], chunks, lam, g_ref[...], lam_init)
    o_ref[...] = o.astype(BF16)


def _attn_sample(qkv, ck, cv, lamp, subln_g, layer, lam_init, dec_batch, dec_seq, h_diff):
    tq = _tile(dec_seq, 512)
    kc = _tile(dec_seq, 1024)
    nq = dec_seq // tq
    past = ck.shape[3]
    ctx = pl.BlockSpec((None, None, None, past, HEAD_W), lambda b, h, i: (b, layer, h, 0, 0))
    return pl.pallas_call(
        functools.partial(_attn_sample_kernel, lam_init=lam_init, kc=kc),
        grid=(dec_batch, h_diff, nq),
        in_specs=[pl.BlockSpec((None, 4, HEAD_W // 2), lambda b, h, i: (layer, 0, 0)),
                  pl.BlockSpec((tq, HEAD_W), lambda b, h, i: (b * nq + i, h)),
                  pl.BlockSpec((dec_seq, HEAD_W), lambda b, h, i: (b, h_diff + h)),
                  pl.BlockSpec((dec_seq, HEAD_W), lambda b, h, i: (b, 2 * h_diff + h)),
                  ctx, ctx,
                  pl.BlockSpec((None, 1, HEAD_W), lambda b, h, i: (layer, 0, 0))],
        out_specs=pl.BlockSpec((tq, HEAD_W), lambda b, h, i: (b * nq + i, h)),
        out_shape=jax.ShapeDtypeStruct((dec_batch * dec_seq, h_diff * HEAD_W), BF16),
        compiler_params=_params(("parallel", "parallel", "arbitrary")),
        name="diff_attn_sample",
    )(lamp, qkv, qkv, qkv, ck, cv, subln_g)


def _ret_kernel(*refs, has_state, emit_state, nc, scale):
    q_ref, k_ref, v_ref, g_ref, if_ref, ib_ref, cols_ref = refs[:7]
    refs = refs[7:]
    if has_state:
        s0_ref, refs = refs[0], refs[1:]
    o_ref, refs = refs[0], refs[1:]
    if emit_state:
        s_ref, refs = refs[0], refs[1:]
    acc_ref = refs[0]
    c_ = RET_CHUNK
    intra_f = if_ref[...]
    intra_b = ib_ref[...]
    cols = cols_ref[...]
    qdf, kdf, qdb, kdb = cols[:, 0:1], cols[:, 1:2], cols[:, 2:3], cols[:, 3:4]
    cdf, cdb = cols[0:1, 4:5], cols[0:1, 5:6]

    acc_ref[...] = jnp.zeros_like(acc_ref)

    def one(c, s, intra, qd, kd, cd):
        rows = pl.ds(pl.multiple_of(c * c_, c_), c_)
        q = q_ref[rows, :]
        k = k_ref[rows, :] * scale
        v = v_ref[rows, :].astype(BF16)
        a = lax.dot_general(q.astype(BF16), k.astype(BF16), _NT, preferred_element_type=F32) * intra
        o = jnp.dot(a.astype(BF16), v, preferred_element_type=F32)
        o = o + jnp.dot((q * qd).astype(BF16), s.astype(BF16), preferred_element_type=F32)
        acc_ref[rows, :] += o
        kt = jnp.transpose(k * kd).astype(BF16)
        return cd * s + jnp.dot(kt, v, preferred_element_type=F32)

    def body(i, carry):
        sf, sb = carry
        sf = one(i, sf, intra_f, qdf, kdf, cdf)
        sb = one(nc - 1 - i, sb, intra_b, qdb, kdb, cdb)
        return sf, sb

    if has_state:
        init = (s0_ref[0], s0_ref[1])
    else:
        init = (jnp.zeros((HEAD_W, HEAD_W), F32), jnp.zeros((HEAD_W, HEAD_W), F32))
    sf, sb = lax.fori_loop(0, nc, body, init)
    if emit_state:
        s_ref[0] = sf
        s_ref[1] = sb

    o = acc_ref[...]
    mu = jnp.mean(o, axis=-1, keepdims=True)
    oc = o - mu
    o = oc * lax.rsqrt(jnp.mean(oc * oc, axis=-1, keepdims=True) + LN_EPS)
    g = g_ref[...]
    o_ref[...] = (o * (g * jax.nn.sigmoid(g))).astype(BF16)


def _retention(z, tabs, state, layer, n_seq, seq, row0, h_diff, h_ret, emit_state):
    intra_f, intra_b, cols = tabs
    c_ = RET_CHUNK
    nc = seq // c_
    assert row0 % seq == 0
    off = row0 // seq
    c0 = 3 * h_diff
    blk = lambda g: pl.BlockSpec((seq, HEAD_W), lambda b, h: (off + b, c0 + g * h_ret + h))
    htab = lambda shape: pl.BlockSpec((None,) + shape, lambda b, h: (h,) + (0,) * len(shape))
    in_specs = [blk(0), blk(1), blk(2), blk(3), htab((c_, c_)), htab((c_, c_)), htab((c_, 8))]
    args = [z, z, z, z, intra_f, intra_b, cols]
    if state is not None:
        in_specs.append(pl.BlockSpec((None, None, 2, None, HEAD_W, HEAD_W),
                                     lambda b, h: (b, layer, 0, h, 0, 0)))
        args.append(state)
    out_specs = [pl.BlockSpec((seq, HEAD_W), lambda b, h: (b, h))]
    out_shape = [jax.ShapeDtypeStruct((n_seq * seq, h_ret * HEAD_W), BF16)]
    if emit_state:
        out_specs.append(pl.BlockSpec((None, 2, None, HEAD_W, HEAD_W), lambda b, h: (b, 0, h, 0, 0)))
        out_shape.append(jax.ShapeDtypeStruct((n_seq, 2, h_ret, HEAD_W, HEAD_W), F32))

    return pl.pallas_call(
        functools.partial(_ret_kernel, has_state=state is not None, emit_state=emit_state, nc=nc,
                          scale=HEAD_W ** -0.5),
        grid=(n_seq, h_ret),
        in_specs=in_specs,
        out_specs=out_specs,
        out_shape=out_shape,
        scratch_shapes=[pltpu.VMEM((seq, HEAD_W), F32)],
        compiler_params=_params(("parallel", "parallel")),
        name="retention_%d" % seq,
    )(*args)


def _layer_norm(v, g, b):
    mu = jnp.mean(v, axis=-1, keepdims=True)
    vc = v - mu
    var = jnp.mean(vc * vc, axis=-1, keepdims=True)
    return vc * lax.rsqrt(var + LN_EPS) * g + b


def _route(p, n_exp):
    per = n_exp // N_GROUPS
    rows = [p[e:e + 1, :] for e in range(n_exp)]
    scores = []
    for g in range(N_GROUPS):
        a, b, c, d = rows[per * g:per * g + per]
        hi1, lo1, hi2, lo2 = jnp.maximum(a, b), jnp.minimum(a, b), jnp.maximum(c, d), jnp.minimum(c, d)
        scores.append(jnp.maximum(hi1, hi2) + jnp.maximum(jnp.minimum(hi1, hi2), jnp.maximum(lo1, lo2)))
    best = scores[0]
    gsel = jnp.zeros(best.shape, jnp.int32)
    for g in range(1, N_GROUPS):
        better = scores[g] > best
        best = jnp.where(better, scores[g], best)
        gsel = jnp.where(better, g, gsel)
    vals = []
    for j in range(per):
        v = rows[(N_GROUPS - 1) * per + j]
        for g in range(N_GROUPS - 2, -1, -1):
            v = jnp.where(gsel == g, rows[g * per + j], v)
        vals.append(v)
    b1, i1 = vals[0], jnp.zeros(best.shape, jnp.int32)
    for j in range(1, per):
        gt = vals[j] > b1
        b1 = jnp.where(gt, vals[j], b1)
        i1 = jnp.where(gt, j, i1)
    b2, i2 = jnp.full(best.shape, -1.0, F32), jnp.zeros(best.shape, jnp.int32)
    for j in range(per):
        gt = jnp.where(i1 == j, -2.0, vals[j]) > b2
        b2 = jnp.where(gt, vals[j], b2)
        i2 = jnp.where(gt, j, i2)
    tot = b1 + b2
    return gsel * per + i1, gsel * per + i2, b1 / tot, b2 / tot


def _outproj_kernel(od_ref, or_ref, wd_ref, wr_ref, x_ref, ga_ref, shf_ref, scf_ref, lng_ref, lnb_ref,
                    wrt_ref, brt_ref, x1_ref, h2_ref, idx_ref, gate_ref, *, alpha):
    mix = jnp.dot(od_ref[...], wd_ref[...], preferred_element_type=F32)
    mix = mix + jnp.dot(or_ref[...], wr_ref[...], preferred_element_type=F32)
    x1 = _layer_norm(alpha * x_ref[...] + ga_ref[...] * mix, lng_ref[...], lnb_ref[...])
    x1_ref[...] = x1
    h2 = x1 * (1.0 + scf_ref[...]) + shf_ref[...]
    h2_ref[...] = h2
    logits = lax.dot_general(wrt_ref[...], h2, _NT, precision=lax.Precision.HIGHEST,
                             preferred_element_type=F32) + brt_ref[...]
    e = jnp.exp(logits - jnp.max(logits, axis=0, keepdims=True))
    p = e / jnp.sum(e, axis=0, keepdims=True)
    e1, e2, g1, g2 = _route(p, logits.shape[0])
    idx_ref[0:1, :] = e1
    idx_ref[1:2, :] = e2
    gate_ref[0:1, :] = g1
    gate_ref[1:2, :] = g2


def _out_proj(o_d, o_r, w_out_b, x, mod6, ln_g, ln_b, w_router_t, b_router_t, layer, tok, alpha):
    t, d = x.shape
    wd = o_d.shape[1]
    wr = o_r.shape[1]
    n_exp = w_router_t.shape[0]
    tm = tok.tile(256)
    row = lambda w: pl.BlockSpec((tm, w), lambda i: (i, 0))
    lnv = pl.BlockSpec((None, None, 1, d), lambda i: (layer, 0, 0, 0))
    return pl.pallas_call(
        functools.partial(_outproj_kernel, alpha=alpha),
        grid=(t // tm,),
        in_specs=[row(wd), row(wr),
                  pl.BlockSpec((None, wd, d), lambda i: (layer, 0, 0)),
                  pl.BlockSpec((None, wr, d), lambda i: (layer, wd // wr, 0)),
                  row(d),
                  tok.mod_spec(layer, 2, tm, d), tok.mod_spec(layer, 3, tm, d), tok.mod_spec(layer, 4, tm, d),
                  lnv, lnv,
                  pl.BlockSpec((n_exp, d), lambda i: (0, 0)),
                  pl.BlockSpec((n_exp, 1), lambda i: (0, 0))],
        out_specs=[row(d), row(d),
                   pl.BlockSpec((TOP_K, tm), lambda i: (0, i)),
                   pl.BlockSpec((TOP_K, tm), lambda i: (0, i))],
        out_shape=[jax.ShapeDtypeStruct((t, d), F32), jax.ShapeDtypeStruct((t, d), F32),
                   jax.ShapeDtypeStruct((TOP_K, t), jnp.int32), jax.ShapeDtypeStruct((TOP_K, t), F32)],
        compiler_params=_params(("parallel",)),
        name="out_proj_ln_router",
    )(o_d, o_r, w_out_b, w_out_b, x, mod6, mod6, mod6, ln_g, ln_b, w_router_t, b_router_t)


def _moe_kernel(be_ref, cnt_ref, tok_ref, dst_ref, gate_ref, x_hbm, w1_ref, w3_ref, w2_ref, y_hbm,
                xbuf, ybuf, gsem, ssem):
    del be_ref
    rows = xbuf.shape[0]
    cnt = cnt_ref[pl.program_id(0)]

    @pl.when(cnt > 0)
    def _():
        def gather(r, c):
            t = tok_ref[0, 0, r]
            pltpu.make_async_copy(x_hbm.at[pl.ds(t, 1), :], xbuf.at[pl.ds(r, 1), :], gsem).start()
            return c

        lax.fori_loop(0, rows, gather, 0)
        pltpu.make_async_copy(x_hbm.at[pl.ds(0, rows), :], xbuf, gsem).wait()

        x = xbuf[...].astype(BF16)
        a = jnp.dot(x, w1_ref[...], preferred_element_type=F32)
        g = jnp.dot(x, w3_ref[...], preferred_element_type=F32)
        h = (a * jax.nn.sigmoid(a) * g).astype(BF16)
        ybuf[...] = jnp.dot(h, w2_ref[...], preferred_element_type=F32) * gate_ref[...]

        def scatter(r, c):
            d = dst_ref[0, 0, r]
            pltpu.make_async_copy(ybuf.at[pl.ds(r, 1), :], y_hbm.at[pl.ds(d, 1), :], ssem).start()
            return c

        def scatter_wait(r, c):
            pltpu.make_async_copy(ybuf.at[pl.ds(0, 1), :], y_hbm.at[pl.ds(0, 1), :], ssem).wait()
            return c

        lax.fori_loop(0, cnt, scatter, 0)
        lax.fori_loop(0, cnt, scatter_wait, 0)


def _experts(h2, block_e, block_cnt, slot_tok, slot_dst, slot_gate, w1b, w3b, w2b, layer, n_rows_out):
    t, d = h2.shape
    nb = block_e.shape[0]
    ff = w1b.shape[-1]
    rows = MOE_ROWS
    smem = pl.BlockSpec((1, 1, rows), lambda b, be, n: (b, 0, 0), memory_space=pltpu.SMEM)
    grid_spec = pltpu.PrefetchScalarGridSpec(
        num_scalar_prefetch=2,
        grid=(nb,),
        in_specs=[smem, smem,
                  pl.BlockSpec((rows, 1), lambda b, be, n: (b, 0)),
                  pl.BlockSpec(memory_space=pl.ANY),
                  pl.BlockSpec((None, None, d, ff), lambda b, be, n: (layer, be[b], 0, 0)),
                  pl.BlockSpec((None, None, d, ff), lambda b, be, n: (layer, be[b], 0, 0)),
                  pl.BlockSpec((None, None, ff, d), lambda b, be, n: (layer, be[b], 0, 0))],
        out_specs=pl.BlockSpec(memory_space=pl.ANY),
        scratch_shapes=[pltpu.VMEM((rows, d), F32), pltpu.VMEM((rows, d), F32),
                        pltpu.SemaphoreType.DMA, pltpu.SemaphoreType.DMA])
    return pl.pallas_call(
        _moe_kernel,
        grid_spec=grid_spec,
        out_shape=jax.ShapeDtypeStruct((n_rows_out, d), F32),
        compiler_params=_params(("arbitrary",)),
        name="moe_experts",
    )(block_e, block_cnt, slot_tok.reshape(nb, 1, rows), slot_dst.reshape(nb, 1, rows),
      slot_gate.reshape(nb * rows, 1), h2, w1b, w3b, w2b)


def _dispatch_plan(idx, gate, n_exp, t):
    rows = MOE_ROWS
    a = t * TOP_K
    flat_e = idx.T.reshape(a)
    flat_g = gate.T.reshape(a)
    order = jnp.argsort(flat_e).astype(jnp.int32)
    counts = jnp.bincount(flat_e, length=n_exp).astype(jnp.int32)
    padded = (counts + rows - 1) // rows * rows
    ends_p = jnp.cumsum(padded)
    starts_p = ends_p - padded
    starts = jnp.cumsum(counts) - counts
    nb = -(-a // rows) + n_exp
    block_e = jnp.clip(jnp.searchsorted(ends_p, jnp.arange(nb, dtype=jnp.int32) * rows, side='right'),
                       0, n_exp - 1).astype(jnp.int32)
    slot = jnp.arange(nb * rows, dtype=jnp.int32)
    e = jnp.repeat(block_e, rows)
    within = slot - starts_p[e]
    valid = within < counts[e]
    src = order[jnp.clip(starts[e] + within, 0, a - 1)]
    slot_tok = jnp.where(valid, src // TOP_K, 0)
    slot_dst = jnp.where(valid, (src % TOP_K) * t + src // TOP_K, 0)
    slot_gate = jnp.where(valid, flat_g[src], 0.0)
    block_cnt = jnp.sum(valid.reshape(nb, rows), axis=1).astype(jnp.int32)
    return block_e, block_cnt, slot_tok, slot_dst, slot_gate


def _ln2_kernel(*refs, alpha, emit_h):
    x1_ref, y0_ref, y1_ref, gf_ref, lng_ref, lnb_ref = refs[:6]
    y = y0_ref[...] + y1_ref[...]
    x2 = _layer_norm(alpha * x1_ref[...] + gf_ref[...] * y, lng_ref[...], lnb_ref[...])
    if emit_h:
        sh_ref, sc_ref, x2_ref, h_ref = refs[6:]
        h_ref[...] = (x2 * (1.0 + sc_ref[...]) + sh_ref[...]).astype(BF16)
    else:
        x2_ref = refs[6]
    x2_ref[...] = x2


def _ln2(x1, y2, mod6, ln_g, ln_b, layer, tok, alpha, last):
    t, d = x1.shape
    tm = tok.tile(512)
    nt = t // tm
    row = lambda off: pl.BlockSpec((tm, d), lambda i: (off + i, 0))
    lnv = pl.BlockSpec((None, None, 1, d), lambda i: (layer, 1, 0, 0))
    in_specs = [row(0), row(0), row(nt), tok.mod_spec(layer, 5, tm, d), lnv, lnv]
    args = [x1, y2, y2, mod6, ln_g, ln_b]
    out_specs = [row(0)]
    out_shape = [jax.ShapeDtypeStruct((t, d), F32)]
    if not last:
        in_specs += [tok.mod_spec(layer + 1, 0, tm, d), tok.mod_spec(layer + 1, 1, tm, d)]
        args += [mod6, mod6]
        out_specs.append(row(0))
        out_shape.append(jax.ShapeDtypeStruct((t, d), BF16))
    return pl.pallas_call(
        functools.partial(_ln2_kernel, alpha=alpha, emit_h=not last),
        grid=(nt,),
        in_specs=in_specs,
        out_specs=out_specs,
        out_shape=out_shape,
        compiler_params=_params(("parallel",)),
        name="ln2_mod",
    )(*args)


def _rope_tables(n_tok):
    rows = n_tok // GRID_W
    axis = HEAD_W // 4
    r = jnp.repeat(jnp.arange(rows, dtype=F32), GRID_W)
    col = jnp.tile(jnp.arange(GRID_W, dtype=F32), rows)
    inv = ROPE_BASE ** (-jnp.arange(0, axis, 2, dtype=F32) / axis)
    ar = r[:, None] * inv[None]
    ac = col[:, None] * inv[None]
    ang = jnp.concatenate([ar, ar, ac, ac], -1)
    cos, sin = jnp.cos(ang), jnp.sin(ang)
    cos, sin = jnp.tile(cos, (1, 2)), jnp.tile(sin, (1, 2))
    low = (jnp.arange(HEAD_W) % (axis)) < axis // 2
    return cos, jnp.where(low, -sin, 0.0), jnp.where(low, 0.0, sin)


def _ret_tables(h_ret):
    c_ = RET_CHUNK
    h = jnp.arange(h_ret, dtype=F32)
    lg_f = jnp.log(1.0 - jnp.exp2(-5.0 - h))
    lg_b = jnp.log(1.0 - jnp.exp2(-5.5 - h))
    pos = jnp.arange(c_, dtype=F32)
    dist = pos[:, None] - pos[None, :]
    intra_f = jnp.where(dist >= 0, jnp.exp(lg_f[:, None, None] * jnp.maximum(dist, 0.0)), 0.0)
    intra_b = jnp.where(dist <= 0, jnp.exp(lg_b[:, None, None] * jnp.maximum(-dist, 0.0)), 0.0)
    qd_f = jnp.exp(lg_f[:, None] * (pos + 1.0))
    kd_f = jnp.exp(lg_f[:, None] * (c_ - 1.0 - pos))
    qd_b = jnp.exp(lg_b[:, None] * (c_ - pos))
    kd_b = jnp.exp(lg_b[:, None] * pos)
    cd_f = jnp.broadcast_to(jnp.exp(lg_f * c_)[:, None], (h_ret, c_))
    cd_b = jnp.broadcast_to(jnp.exp(lg_b * c_)[:, None], (h_ret, c_))
    zero = jnp.zeros((h_ret, c_), F32)
    cols = jnp.stack([qd_f, kd_f, qd_b, kd_b, cd_f, cd_b, zero, zero], axis=2)
    return intra_f, intra_b, cols


def kernel(x_prompt, x_sample, c, cache_k, cache_v, state_ret, c_ctx, w_mod, b_mod, w_in, w_out,
           lam_q1, lam_k1, lam_q2, lam_k2, subln_g, ln_g, ln_b, w_router, b_router, w1, w3, w2):
    batch, seq, d = x_prompt.shape
    dec_batch, dec_seq, _ = x_sample.shape
    depth = w_mod.shape[0]
    h_diff = cache_k.shape[2]
    h_ret = state_ret.shape[3]
    n_exp = w1.shape[1]
    assert cache_k.shape[-1] == cache_v.shape[-1] == HEAD_W
    assert state_ret.shape[-2:] == (HEAD_W, HEAD_W)
    alpha = (2 * depth) ** 0.25

    n_cond = 8
    tok = _Tokens(batch, seq, dec_batch, dec_seq, n_cond)
    cs = jnp.concatenate([c_ctx[None], c, jnp.zeros((n_cond - 1 - dec_batch, d), F32)], 0)
    mod6 = _modulation(cs, w_mod, b_mod).reshape(depth * n_cond * 6, 1, d)

    w_in_b = w_in.astype(BF16)
    w_out_b = w_out.astype(BF16)
    w1b, w3b, w2b = w1.astype(BF16), w3.astype(BF16), w2.astype(BF16)
    ck_b, cv_b = cache_k.astype(BF16), cache_v.astype(BF16)
    lamp = jnp.stack([lam_q1, lam_k1, lam_q2, lam_k2], axis=1)
    subln = subln_g.reshape(depth, 1, HEAD_W)
    ln_g4 = ln_g.reshape(depth, 2, 1, d)
    ln_b4 = ln_b.reshape(depth, 2, 1, d)
    w_router_t = w_router.T
    b_router_t = b_router.reshape(n_exp, 1)
    rope = _rope_tables(dec_seq)
    ret_tabs = _ret_tables(h_ret)

    x = jnp.concatenate([x_prompt.reshape(tok.t_p, d), x_sample.reshape(tok.t_s, d)], 0)
    h = _premod(x, mod6, tok)
    ks_out, vs_out, ss_out = [], [], []
    for layer in range(depth):
        lam_init = 0.8 - 0.6 * math.exp(-0.3 * layer)
        z = _in_proj(h, w_in_b, layer)
        od_p, own_k, own_v = _attn_prompt(z, lamp, subln, layer, lam_init, batch, seq, h_diff)
        qkv = _rope_prep(z, rope, tok, h_diff)
        od_s = _attn_sample(qkv, ck_b, cv_b, lamp, subln, layer, lam_init, dec_batch, dec_seq, h_diff)
        or_p, s_new = _retention(z, ret_tabs, None, layer, batch, seq, 0, h_diff, h_ret, True)
        (or_s,) = _retention(z, ret_tabs, state_ret, layer, dec_batch, dec_seq, tok.t_p, h_diff, h_ret, False)
        ks_out.append(own_k)
        vs_out.append(own_v)
        ss_out.append(s_new)
        o_d = jnp.concatenate([od_p, od_s], 0)
        o_r = jnp.concatenate([or_p, or_s], 0)
        x1, h2, idx, gate = _out_proj(o_d, o_r, w_out_b, x, mod6, ln_g4, ln_b4, w_router_t, b_router_t,
                                      layer, tok, alpha)
        plan = _dispatch_plan(idx, gate, n_exp, tok.t)
        y2 = _experts(h2, *plan, w1b, w3b, w2b, layer, TOP_K * tok.t)
        last = layer == depth - 1
        outs = _ln2(x1, y2, mod6, ln_g4, ln_b4, layer, tok, alpha, last)
        x = outs[0]
        if not last:
            h = outs[1]
    xp = x[:tok.t_p].reshape(batch, seq, d)
    xs = x[tok.t_p:].reshape(dec_batch, dec_seq, d)
    return (xp, xs, jnp.stack(ks_out, axis=1), jnp.stack(vs_out, axis=1), jnp.stack(ss_out, axis=1))
```

```python
import functools
import math

import jax
import jax.numpy as jnp
from jax import lax
from jax.experimental import pallas as pl
from jax.experimental.pallas import tpu as pltpu

F32 = jnp.float32
BF16 = jnp.bfloat16

GRID_W = 64
ROPE_BASE = 10000.0
RET_CHUNK = 128
N_GROUPS = 4
TOP_K = 2
LN_EPS = 1e-5
HEAD_W = 128
MOE_ROWS = 256
VMEM_LIMIT = 56 * 1024 * 1024

_NT = (((1,), (1,)), ((), ()))
Q_SCALE = 0.125 * math.log2(math.e)


def _tile(n, pref, mult=8):
    t = min(n, pref)
    while n % t or (t % mult and t != n):
        t -= 1
    return t


def _params(sem, vmem=VMEM_LIMIT):
    return pltpu.CompilerParams(dimension_semantics=sem, vmem_limit_bytes=vmem)


def _mod_kernel(c_ref, w_ref, b_ref, o_ref):
    c = c_ref[...]
    s = c * jax.nn.sigmoid(c)
    o_ref[...] = jnp.dot(s, w_ref[...], precision=lax.Precision.HIGHEST,
                         preferred_element_type=F32) + b_ref[...]


def _modulation(cs, w_mod, b_mod):
    depth, d, n = w_mod.shape
    rows = cs.shape[0]
    tn = _tile(n, 1024, 128)
    return pl.pallas_call(
        _mod_kernel,
        grid=(depth, n // tn),
        in_specs=[pl.BlockSpec((rows, d), lambda l, j: (0, 0)),
                  pl.BlockSpec((None, d, tn), lambda l, j: (l, 0, j)),
                  pl.BlockSpec((None, 1, tn), lambda l, j: (l, 0, j))],
        out_specs=pl.BlockSpec((None, rows, tn), lambda l, j: (l, 0, j)),
        out_shape=jax.ShapeDtypeStruct((depth, rows, n), F32),
        compiler_params=_params(("parallel", "parallel")),
        name="adaln_mod",
    )(cs, w_mod, b_mod.reshape(depth, 1, n))


class _Tokens:
    def __init__(self, batch, seq, dec_batch, dec_seq, n_cond):
        self.t_p = batch * seq
        self.t_s = dec_batch * dec_seq
        self.t = self.t_p + self.t_s
        self.dec_seq = dec_seq
        self.n_cond = n_cond

    def tile(self, pref):
        return _tile(math.gcd(self.t_p, self.dec_seq), pref)

    def mod_spec(self, layer, which, tm, d):
        t_p, dec_seq, n_cond = self.t_p, self.dec_seq, self.n_cond

        def idx(i, *_):
            r = i * tm
            cond = jnp.where(r < t_p, 0, 1 + (r - t_p) // dec_seq)
            return ((layer * n_cond + cond) * 6 + which, 0, 0)

        return pl.BlockSpec((None, 1, d), idx)


def _vec_spec(d):
    return pl.BlockSpec((1, d), lambda i, *_: (0, 0))


def _premod_kernel(x_ref, sh_ref, sc_ref, h_ref):
    h_ref[...] = (x_ref[...] * (1.0 + sc_ref[...]) + sh_ref[...]).astype(BF16)


def _premod(x, mod6, tok):
    t, d = x.shape
    tm = tok.tile(512)
    return pl.pallas_call(
        _premod_kernel,
        grid=(t // tm,),
        in_specs=[pl.BlockSpec((tm, d), lambda i: (i, 0)),
                  tok.mod_spec(0, 0, tm, d), tok.mod_spec(0, 1, tm, d)],
        out_specs=pl.BlockSpec((tm, d), lambda i: (i, 0)),
        out_shape=jax.ShapeDtypeStruct((t, d), BF16),
        compiler_params=_params(("parallel",)),
        name="premod",
    )(x, mod6, mod6)


def _matmul_kernel(a_ref, w_ref, o_ref):
    o_ref[...] = jnp.dot(a_ref[...], w_ref[...], preferred_element_type=F32)


def _in_proj(h, w_in_b, layer):
    t, d = h.shape
    n = w_in_b.shape[-1]
    tm = _tile(t, 1024)
    tn = _tile(n, 1024, 128)
    return pl.pallas_call(
        _matmul_kernel,
        grid=(t // tm, n // tn),
        in_specs=[pl.BlockSpec((tm, d), lambda i, j: (i, 0)),
                  pl.BlockSpec((None, d, tn), lambda i, j: (layer, 0, j))],
        out_specs=pl.BlockSpec((tm, tn), lambda i, j: (i, j)),
        out_shape=jax.ShapeDtypeStruct((t, n), F32),
        compiler_params=_params(("parallel", "parallel")),
        name="in_proj",
    )(h, w_in_b)


def _prep_kernel(z_ref, cos_ref, sa_ref, sb_ref, o_ref, *, n_rot, n_q):
    cos, sa, sb = cos_ref[...], sa_ref[...], sb_ref[...]
    for j in range(n_rot):
        x = z_ref[:, j * HEAD_W:(j + 1) * HEAD_W]
        r = x * cos + pltpu.roll(x, HEAD_W - 16, 1) * sa + pltpu.roll(x, 16, 1) * sb
        if j < n_q:
            r = r * Q_SCALE
        o_ref[:, j * HEAD_W:(j + 1) * HEAD_W] = r.astype(BF16)
    ones = jnp.ones((z_ref.shape[0], HEAD_W), BF16)
    for j in range(n_q):
        c0 = (n_rot + 2 * j) * HEAD_W
        o_ref[:, c0:c0 + HEAD_W] = z_ref[:, (n_rot + j) * HEAD_W:(n_rot + j + 1) * HEAD_W].astype(BF16)
        o_ref[:, c0 + HEAD_W:c0 + 2 * HEAD_W] = ones


def _rope_prep(z, tables, tok, h_diff):
    cos, sa, sb = tables
    w = 3 * h_diff * HEAD_W
    wo = 4 * h_diff * HEAD_W
    tm = _tile(tok.dec_seq, 256)
    off = tok.t_p // tm
    nseq = tok.dec_seq // tm
    tab = pl.BlockSpec((tm, HEAD_W), lambda i: (i % nseq, 0))
    return pl.pallas_call(
        functools.partial(_prep_kernel, n_rot=2 * h_diff, n_q=h_diff),
        grid=(tok.t_s // tm,),
        in_specs=[pl.BlockSpec((tm, w), lambda i: (off + i, 0)), tab, tab, tab],
        out_specs=pl.BlockSpec((tm, wo), lambda i: (i, 0)),
        out_shape=jax.ShapeDtypeStruct((tok.t_s, wo), BF16),
        compiler_params=_params(("parallel",)),
        name="rope_prep",
    )(z, cos, sa, sb)


def _lam_value(lamp_ref, lam_init):
    lp = lamp_ref[...]
    a = jnp.sum(lp[0:1] * lp[1:2], axis=-1, keepdims=True)
    b = jnp.sum(lp[2:3] * lp[3:4], axis=-1, keepdims=True)
    return jnp.exp(a) - jnp.exp(b) + lam_init


def _softmax_pv(qc, chunks):
    m = acc = None
    for kk, vv in chunks:
        s = lax.dot_general(qc, kk, _NT, preferred_element_type=F32)
        cm = jnp.max(s, axis=-1, keepdims=True)
        m_new = cm if m is None else jnp.maximum(m, cm)
        pv = jnp.dot(jnp.exp2(s - m_new).astype(BF16), vv, preferred_element_type=F32)
        acc = pv if m is None else jnp.exp2(m - m_new) * acc + pv
        m = m_new
    return acc[:, :HEAD_W] / acc[:, HEAD_W:]


def _diff_out(q, chunks, lam, g, lam_init):
    lane = lax.broadcasted_iota(jnp.int32, q.shape, 1)
    zero = jnp.zeros_like(q)
    o1 = _softmax_pv(jnp.where(lane < HEAD_W // 2, q, zero), chunks)
    o2 = _softmax_pv(jnp.where(lane >= HEAD_W // 2, q, zero), chunks)
    o = o1 - lam * o2
    o = o * lax.rsqrt(jnp.mean(o * o, axis=-1, keepdims=True) + LN_EPS) * g
    return o * (1.0 - lam_init)


def _attn_prompt_kernel(lamp_ref, q_ref, k_ref, v_ref, g_ref, o_ref, ok_ref, ov_ref, *, lam_init):
    k = k_ref[...]
    v = v_ref[...]
    ok_ref[...] = k
    ov_ref[...] = v
    q = (q_ref[...] * Q_SCALE).astype(BF16)
    lam = _lam_value(lamp_ref, lam_init)
    v1 = jnp.concatenate([v.astype(BF16), jnp.ones(v.shape, BF16)], axis=1)
    o = _diff_out(q, [(k.astype(BF16), v1)], lam, g_ref[...], lam_init)
    o_ref[...] = o.astype(BF16)


def _attn_prompt(z, lamp, subln_g, layer, lam_init, batch, seq, h_diff):
    t_p = batch * seq
    blk = lambda c0: pl.BlockSpec((seq, HEAD_W), lambda b, h: (b, c0 + h))
    own = pl.BlockSpec((None, None, seq, HEAD_W), lambda b, h: (b, h, 0, 0))
    return pl.pallas_call(
        functools.partial(_attn_prompt_kernel, lam_init=lam_init),
        grid=(batch, h_diff),
        in_specs=[pl.BlockSpec((None, 4, HEAD_W // 2), lambda b, h: (layer, 0, 0)),
                  blk(0), blk(h_diff), blk(2 * h_diff),
                  pl.BlockSpec((None, 1, HEAD_W), lambda b, h: (layer, 0, 0))],
        out_specs=[pl.BlockSpec((seq, HEAD_W), lambda b, h: (b, h)), own, own],
        out_shape=[jax.ShapeDtypeStruct((t_p, h_diff * HEAD_W), BF16),
                   jax.ShapeDtypeStruct((batch, h_diff, seq, HEAD_W), F32),
                   jax.ShapeDtypeStruct((batch, h_diff, seq, HEAD_W), F32)],
        compiler_params=_params(("parallel", "parallel")),
        name="diff_attn_prompt",
    )(lamp, z, z, z, subln_g)


def _attn_sample_kernel(lamp_ref, q_ref, k_ref, v_ref, ck_ref, cv_ref, g_ref, o_ref, *, lam_init, kc):
    n = k_ref.shape[0]
    chunks = [(ck_ref[...], cv_ref[...])]
    for c in range(n // kc):
        chunks.append((k_ref[c * kc:(c + 1) * kc, :], v_ref[c * kc:(c + 1) * kc, :]))
    lam = _lam_value(lamp_ref, lam_init)
    o = _diff_out(q_ref[...], chunks, lam, g_ref[...], lam_init)
    o_ref[...] = o.astype(BF16)


def _attn_sample(qkv, ck, cv, lamp, subln_g, layer, lam_init, dec_batch, dec_seq, h_diff):
    tq = _tile(dec_seq, 512)
    kc = _tile(dec_seq, 1024)
    nq = dec_seq // tq
    past = ck.shape[3]
    ctx = lambda w: pl.BlockSpec((None, None, None, past, w), lambda b, h, i: (b, layer, h, 0, 0))
    return pl.pallas_call(
        functools.partial(_attn_sample_kernel, lam_init=lam_init, kc=kc),
        grid=(dec_batch, h_diff, nq),
        in_specs=[pl.BlockSpec((None, 4, HEAD_W // 2), lambda b, h, i: (layer, 0, 0)),
                  pl.BlockSpec((tq, HEAD_W), lambda b, h, i: (b * nq + i, h)),
                  pl.BlockSpec((dec_seq, HEAD_W), lambda b, h, i: (b, h_diff + h)),
                  pl.BlockSpec((dec_seq, 2 * HEAD_W), lambda b, h, i: (b, h_diff + h)),
                  ctx(HEAD_W), ctx(2 * HEAD_W),
                  pl.BlockSpec((None, 1, HEAD_W), lambda b, h, i: (layer, 0, 0))],
        out_specs=pl.BlockSpec((tq, HEAD_W), lambda b, h, i: (b * nq + i, h)),
        out_shape=jax.ShapeDtypeStruct((dec_batch * dec_seq, h_diff * HEAD_W), BF16),
        compiler_params=_params(("parallel", "parallel", "arbitrary")),
        name="diff_attn_sample",
    )(lamp, qkv, qkv, qkv, ck, cv, subln_g)


def _ret_kernel(*refs, has_state, emit_state, nc, scale):
    q_ref, k_ref, v_ref, g_ref, if_ref, ib_ref, cols_ref = refs[:7]
    refs = refs[7:]
    if has_state:
        s0_ref, refs = refs[0], refs[1:]
    o_ref, refs = refs[0], refs[1:]
    if emit_state:
        s_ref, refs = refs[0], refs[1:]
    of_ref, ob_ref, sf_ref, sb_ref = refs
    c_ = RET_CHUNK
    intra_f = if_ref[...]
    intra_b = ib_ref[...]
    cols = cols_ref[...]
    qdf, kdf, qdb, kdb = cols[:, 0:1], cols[:, 1:2], cols[:, 2:3], cols[:, 3:4]
    cdf, cdb = cols[0:1, 4:5], cols[0:1, 5:6]

    def chunk_rows(c):
        return pl.ds(pl.multiple_of(c * c_, c_), c_)

    def intra_and_state(c, s, intra, kd, cd, out_ref, st_ref):
        rows = chunk_rows(c)
        st_ref[c] = s.astype(BF16)
        q = q_ref[rows, :].astype(BF16)
        k = k_ref[rows, :] * scale
        v = v_ref[rows, :].astype(BF16)
        a = lax.dot_general(q, k.astype(BF16), _NT, preferred_element_type=F32) * intra
        out_ref[rows, :] = jnp.dot(a.astype(BF16), v, preferred_element_type=F32)
        kt = jnp.transpose(k * kd).astype(BF16)
        return cd * s + jnp.dot(kt, v, preferred_element_type=F32)

    def scan(i, carry):
        sf, sb = carry
        sf = intra_and_state(i, sf, intra_f, kdf, cdf, of_ref, sf_ref)
        sb = intra_and_state(nc - 1 - i, sb, intra_b, kdb, cdb, ob_ref, sb_ref)
        return sf, sb

    if has_state:
        init = (s0_ref[0], s0_ref[1])
    else:
        init = (jnp.zeros((HEAD_W, HEAD_W), F32), jnp.zeros((HEAD_W, HEAD_W), F32))
    unroll = _tile(nc, 4, 1)
    sf, sb = lax.fori_loop(0, nc, scan, init, unroll=unroll)
    if emit_state:
        s_ref[0] = sf
        s_ref[1] = sb

    def finish(c, carry):
        rows = chunk_rows(c)
        q = q_ref[rows, :]
        o = of_ref[rows, :] + ob_ref[rows, :]
        o = o + jnp.dot((q * qdf).astype(BF16), sf_ref[c], preferred_element_type=F32)
        o = o + jnp.dot((q * qdb).astype(BF16), sb_ref[c], preferred_element_type=F32)
        mu = jnp.mean(o, axis=-1, keepdims=True)
        oc = o - mu
        o = oc * lax.rsqrt(jnp.mean(oc * oc, axis=-1, keepdims=True) + LN_EPS)
        g = g_ref[rows, :]
        o_ref[rows, :] = (o * (g * jax.nn.sigmoid(g))).astype(BF16)
        return carry

    lax.fori_loop(0, nc, finish, 0, unroll=unroll)


def _retention(z, tabs, state, layer, n_seq, seq, row0, h_diff, h_ret, emit_state):
    intra_f, intra_b, cols = tabs
    c_ = RET_CHUNK
    nc = seq // c_
    assert row0 % seq == 0
    off = row0 // seq
    c0 = 3 * h_diff
    blk = lambda g: pl.BlockSpec((seq, HEAD_W), lambda b, h: (off + b, c0 + g * h_ret + h))
    htab = lambda shape: pl.BlockSpec((None,) + shape, lambda b, h: (h,) + (0,) * len(shape))
    in_specs = [blk(0), blk(1), blk(2), blk(3), htab((c_, c_)), htab((c_, c_)), htab((c_, 8))]
    args = [z, z, z, z, intra_f, intra_b, cols]
    if state is not None:
        in_specs.append(pl.BlockSpec((None, None, 2, None, HEAD_W, HEAD_W),
                                     lambda b, h: (b, layer, 0, h, 0, 0)))
        args.append(state)
    out_specs = [pl.BlockSpec((seq, HEAD_W), lambda b, h: (b, h))]
    out_shape = [jax.ShapeDtypeStruct((n_seq * seq, h_ret * HEAD_W), BF16)]
    if emit_state:
        out_specs.append(pl.BlockSpec((None, 2, None, HEAD_W, HEAD_W), lambda b, h: (b, 0, h, 0, 0)))
        out_shape.append(jax.ShapeDtypeStruct((n_seq, 2, h_ret, HEAD_W, HEAD_W), F32))

    return pl.pallas_call(
        functools.partial(_ret_kernel, has_state=state is not None, emit_state=emit_state, nc=nc,
                          scale=HEAD_W ** -0.5),
        grid=(n_seq, h_ret),
        in_specs=in_specs,
        out_specs=out_specs,
        out_shape=out_shape,
        scratch_shapes=[pltpu.VMEM((seq, HEAD_W), F32), pltpu.VMEM((seq, HEAD_W), F32),
                        pltpu.VMEM((nc, HEAD_W, HEAD_W), BF16), pltpu.VMEM((nc, HEAD_W, HEAD_W), BF16)],
        compiler_params=_params(("parallel", "parallel")),
        name="retention_%d" % seq,
    )(*args)


def _layer_norm(v, g, b):
    mu = jnp.mean(v, axis=-1, keepdims=True)
    vc = v - mu
    var = jnp.mean(vc * vc, axis=-1, keepdims=True)
    return vc * lax.rsqrt(var + LN_EPS) * g + b


def _route(p, n_exp):
    per = n_exp // N_GROUPS
    rows = [p[e:e + 1, :] for e in range(n_exp)]
    scores = []
    for g in range(N_GROUPS):
        a, b, c, d = rows[per * g:per * g + per]
        hi1, lo1, hi2, lo2 = jnp.maximum(a, b), jnp.minimum(a, b), jnp.maximum(c, d), jnp.minimum(c, d)
        scores.append(jnp.maximum(hi1, hi2) + jnp.maximum(jnp.minimum(hi1, hi2), jnp.maximum(lo1, lo2)))
    best = scores[0]
    gsel = jnp.zeros(best.shape, jnp.int32)
    for g in range(1, N_GROUPS):
        better = scores[g] > best
        best = jnp.where(better, scores[g], best)
        gsel = jnp.where(better, g, gsel)
    vals = []
    for j in range(per):
        v = rows[(N_GROUPS - 1) * per + j]
        for g in range(N_GROUPS - 2, -1, -1):
            v = jnp.where(gsel == g, rows[g * per + j], v)
        vals.append(v)
    b1, i1 = vals[0], jnp.zeros(best.shape, jnp.int32)
    for j in range(1, per):
        gt = vals[j] > b1
        b1 = jnp.where(gt, vals[j], b1)
        i1 = jnp.where(gt, j, i1)
    b2, i2 = jnp.full(best.shape, -1.0, F32), jnp.zeros(best.shape, jnp.int32)
    for j in range(per):
        gt = jnp.where(i1 == j, -2.0, vals[j]) > b2
        b2 = jnp.where(gt, vals[j], b2)
        i2 = jnp.where(gt, j, i2)
    tot = b1 + b2
    return gsel * per + i1, gsel * per + i2, b1 / tot, b2 / tot


def _outproj_kernel(od_ref, or_ref, wd_ref, wr_ref, x_ref, ga_ref, shf_ref, scf_ref, lng_ref, lnb_ref,
                    wrt_ref, brt_ref, x1_ref, h2_ref, idx_ref, gate_ref, *, alpha):
    mix = jnp.dot(od_ref[...], wd_ref[...], preferred_element_type=F32)
    mix = mix + jnp.dot(or_ref[...], wr_ref[...], preferred_element_type=F32)
    x1 = _layer_norm(alpha * x_ref[...] + ga_ref[...] * mix, lng_ref[...], lnb_ref[...])
    x1_ref[...] = x1
    h2 = x1 * (1.0 + scf_ref[...]) + shf_ref[...]
    h2_ref[...] = h2
    h_hi = h2.astype(BF16)
    h_lo = (h2 - h_hi.astype(F32)).astype(BF16)
    w_hi, w_lo = wrt_ref[0], wrt_ref[1]
    logits = (lax.dot_general(w_hi, h_hi, _NT, preferred_element_type=F32)
              + lax.dot_general(w_lo, h_hi, _NT, preferred_element_type=F32)
              + lax.dot_general(w_hi, h_lo, _NT, preferred_element_type=F32)) + brt_ref[...]
    e = jnp.exp(logits - jnp.max(logits, axis=0, keepdims=True))
    p = e / jnp.sum(e, axis=0, keepdims=True)
    e1, e2, g1, g2 = _route(p, logits.shape[0])
    idx_ref[0:1, :] = e1
    idx_ref[1:2, :] = e2
    gate_ref[0:1, :] = g1
    gate_ref[1:2, :] = g2


def _out_proj(o_d, o_r, w_out_b, x, mod6, ln_g, ln_b, w_router_t, b_router_t, layer, tok, alpha):
    t, d = x.shape
    wd = o_d.shape[1]
    wr = o_r.shape[1]
    n_exp = w_router_t.shape[1]
    tm = tok.tile(512)
    row = lambda w: pl.BlockSpec((tm, w), lambda i: (i, 0))
    lnv = pl.BlockSpec((None, None, 1, d), lambda i: (layer, 0, 0, 0))
    return pl.pallas_call(
        functools.partial(_outproj_kernel, alpha=alpha),
        grid=(t // tm,),
        in_specs=[row(wd), row(wr),
                  pl.BlockSpec((None, wd, d), lambda i: (layer, 0, 0)),
                  pl.BlockSpec((None, wr, d), lambda i: (layer, wd // wr, 0)),
                  row(d),
                  tok.mod_spec(layer, 2, tm, d), tok.mod_spec(layer, 3, tm, d), tok.mod_spec(layer, 4, tm, d),
                  lnv, lnv,
                  pl.BlockSpec((2, n_exp, d), lambda i: (0, 0, 0)),
                  pl.BlockSpec((n_exp, 1), lambda i: (0, 0))],
        out_specs=[row(d), row(d),
                   pl.BlockSpec((TOP_K, tm), lambda i: (0, i)),
                   pl.BlockSpec((TOP_K, tm), lambda i: (0, i))],
        out_shape=[jax.ShapeDtypeStruct((t, d), F32), jax.ShapeDtypeStruct((t, d), F32),
                   jax.ShapeDtypeStruct((TOP_K, t), jnp.int32), jax.ShapeDtypeStruct((TOP_K, t), F32)],
        compiler_params=_params(("parallel",)),
        name="out_proj_ln_router",
    )(o_d, o_r, w_out_b, w_out_b, x, mod6, mod6, mod6, ln_g, ln_b, w_router_t, b_router_t)


def _moe_kernel(be_ref, cnt_ref, tok0_ref, tokn_ref, dst_ref, gate_ref, x_hbm, w1_ref, w3_ref, w2_ref, y_hbm,
                xbuf, ybuf, gsem, ssem, *, n_out):
    del be_ref
    s = pl.program_id(0)
    rows = xbuf.shape[1]
    c_old, c_cur, c_next = cnt_ref[s], cnt_ref[s + 2], cnt_ref[s + 3]

    def gather_copy(tok_ref, r, p):
        return pltpu.make_async_copy(x_hbm.at[pl.ds(tok_ref[0, 0, r], 1), :], xbuf.at[p, pl.ds(r, 1), :], gsem.at[p])

    def scatter_copy(r, p):
        return pltpu.make_async_copy(ybuf.at[p, pl.ds(r, 1), :], y_hbm.at[pl.ds(dst_ref[0, 0, r], 1), :], ssem.at[p])

    @pl.when(s == 0)
    def _():
        for p in range(2):
            ybuf[p] = jnp.zeros(ybuf.shape[1:], F32)
            spare = pltpu.make_async_copy(ybuf.at[p], y_hbm.at[pl.ds(n_out + p * rows, rows), :], ssem.at[p])
            spare.start()
            spare.wait()

        def first(r, c):
            gather_copy(tok0_ref, r, 0).start()
            return c

        lax.fori_loop(0, rows, first, 0)

    for p in range(2):
        mine = s % 2 == p

        @pl.when(jnp.logical_and(mine, c_old > 0))
        def _():
            pltpu.make_async_copy(ybuf.at[p], y_hbm.at[pl.ds(0, rows), :], ssem.at[p]).wait()

        @pl.when(jnp.logical_and(mine, c_next > 0))
        def _():
            for r in range(rows):
                gather_copy(tokn_ref, r, 1 - p).start()

        @pl.when(jnp.logical_and(mine, c_cur > 0))
        def _():
            pltpu.make_async_copy(x_hbm.at[pl.ds(0, rows), :], xbuf.at[p], gsem.at[p]).wait()
            x = xbuf[p].astype(BF16)
            a = jnp.dot(x, w1_ref[...], preferred_element_type=F32)
            g = jnp.dot(x, w3_ref[...], preferred_element_type=F32)
            h = (a * jax.nn.sigmoid(a) * g).astype(BF16)
            ybuf[p] = jnp.dot(h, w2_ref[...], preferred_element_type=F32) * gate_ref[...]
            for r in range(rows):
                scatter_copy(r, p).start()


def _experts(h2, block_e, block_cnt, slot_tok, slot_dst, slot_gate, w1b, w3b, w2b, layer, n_out):
    t, d = h2.shape
    nb = block_e.shape[0]
    ff = w1b.shape[-1]
    rows = MOE_ROWS
    last = nb - 1
    cnt_pad = jnp.concatenate([jnp.zeros((2,), jnp.int32), block_cnt, jnp.zeros((3,), jnp.int32)])
    smem = lambda off: pl.BlockSpec((1, 1, rows), lambda s, be, n: (jnp.minimum(s + off, last), 0, 0),
                                    memory_space=pltpu.SMEM)
    wspec = lambda shape: pl.BlockSpec((None, None) + shape,
                                       lambda s, be, n: (layer, be[jnp.minimum(s, last)], 0, 0))
    grid_spec = pltpu.PrefetchScalarGridSpec(
        num_scalar_prefetch=2,
        grid=(nb + 2,),
        in_specs=[smem(0), smem(1), smem(0),
                  pl.BlockSpec((rows, 1), lambda s, be, n: (jnp.minimum(s, last), 0)),
                  pl.BlockSpec(memory_space=pl.ANY),
                  wspec((d, ff)), wspec((d, ff)), wspec((ff, d))],
        out_specs=pl.BlockSpec(memory_space=pl.ANY),
        scratch_shapes=[pltpu.VMEM((2, rows, d), F32), pltpu.VMEM((2, rows, d), F32),
                        pltpu.SemaphoreType.DMA((2,)), pltpu.SemaphoreType.DMA((2,))])
    tok3 = slot_tok.reshape(nb, 1, rows)
    return pl.pallas_call(
        functools.partial(_moe_kernel, n_out=n_out),
        grid_spec=grid_spec,
        out_shape=jax.ShapeDtypeStruct((n_out + 2 * rows, d), F32),
        compiler_params=_params(("arbitrary",)),
        name="moe_experts",
    )(block_e, cnt_pad, tok3, tok3, slot_dst.reshape(nb, 1, rows), slot_gate.reshape(nb * rows, 1),
      h2, w1b, w3b, w2b)


def _dispatch_plan(idx, gate, n_exp, t):
    rows = MOE_ROWS
    a = t * TOP_K
    flat_e = idx.T.reshape(a)
    flat_g = gate.T.reshape(a)
    order = jnp.argsort(flat_e).astype(jnp.int32)
    counts = jnp.bincount(flat_e, length=n_exp).astype(jnp.int32)
    padded = (counts + rows - 1) // rows * rows
    ends_p = jnp.cumsum(padded)
    starts_p = ends_p - padded
    starts = jnp.cumsum(counts) - counts
    nb = -(-a // rows) + n_exp
    block_e = jnp.clip(jnp.searchsorted(ends_p, jnp.arange(nb, dtype=jnp.int32) * rows, side='right'),
                       0, n_exp - 1).astype(jnp.int32)
    slot = jnp.arange(nb * rows, dtype=jnp.int32)
    e = jnp.repeat(block_e, rows)
    within = slot - starts_p[e]
    valid = within < counts[e]
    src = order[jnp.clip(starts[e] + within, 0, a - 1)]
    slot_tok = jnp.where(valid, src // TOP_K, 0)
    spare = a + ((slot // rows) % 2) * rows + slot % rows
    slot_dst = jnp.where(valid, (src % TOP_K) * t + src // TOP_K, spare)
    slot_gate = jnp.where(valid, flat_g[src], 0.0)
    block_cnt = jnp.sum(valid.reshape(nb, rows), axis=1).astype(jnp.int32)
    return block_e, block_cnt, slot_tok, slot_dst, slot_gate


def _ln2_kernel(*refs, alpha, emit_h):
    x1_ref, y0_ref, y1_ref, gf_ref, lng_ref, lnb_ref = refs[:6]
    y = y0_ref[...] + y1_ref[...]
    x2 = _layer_norm(alpha * x1_ref[...] + gf_ref[...] * y, lng_ref[...], lnb_ref[...])
    if emit_h:
        sh_ref, sc_ref, x2_ref, h_ref = refs[6:]
        h_ref[...] = (x2 * (1.0 + sc_ref[...]) + sh_ref[...]).astype(BF16)
    else:
        x2_ref = refs[6]
    x2_ref[...] = x2


def _ln2(x1, y2, mod6, ln_g, ln_b, layer, tok, alpha, last):
    t, d = x1.shape
    tm = tok.tile(512)
    nt = t // tm
    row = lambda off: pl.BlockSpec((tm, d), lambda i: (off + i, 0))
    lnv = pl.BlockSpec((None, None, 1, d), lambda i: (layer, 1, 0, 0))
    in_specs = [row(0), row(0), row(nt), tok.mod_spec(layer, 5, tm, d), lnv, lnv]
    args = [x1, y2, y2, mod6, ln_g, ln_b]
    out_specs = [row(0)]
    out_shape = [jax.ShapeDtypeStruct((t, d), F32)]
    if not last:
        in_specs += [tok.mod_spec(layer + 1, 0, tm, d), tok.mod_spec(layer + 1, 1, tm, d)]
        args += [mod6, mod6]
        out_specs.append(row(0))
        out_shape.append(jax.ShapeDtypeStruct((t, d), BF16))
    return pl.pallas_call(
        functools.partial(_ln2_kernel, alpha=alpha, emit_h=not last),
        grid=(nt,),
        in_specs=in_specs,
        out_specs=out_specs,
        out_shape=out_shape,
        compiler_params=_params(("parallel",)),
        name="ln2_mod",
    )(*args)


def _rope_tables(n_tok):
    rows = n_tok // GRID_W
    axis = HEAD_W // 4
    r = jnp.repeat(jnp.arange(rows, dtype=F32), GRID_W)
    col = jnp.tile(jnp.arange(GRID_W, dtype=F32), rows)
    inv = ROPE_BASE ** (-jnp.arange(0, axis, 2, dtype=F32) / axis)
    ar = r[:, None] * inv[None]
    ac = col[:, None] * inv[None]
    ang = jnp.concatenate([ar, ar, ac, ac], -1)
    cos, sin = jnp.cos(ang), jnp.sin(ang)
    cos, sin = jnp.tile(cos, (1, 2)), jnp.tile(sin, (1, 2))
    low = (jnp.arange(HEAD_W) % (axis)) < axis // 2
    return cos, jnp.where(low, -sin, 0.0), jnp.where(low, 0.0, sin)


def _ret_tables(h_ret):
    c_ = RET_CHUNK
    h = jnp.arange(h_ret, dtype=F32)
    lg_f = jnp.log(1.0 - jnp.exp2(-5.0 - h))
    lg_b = jnp.log(1.0 - jnp.exp2(-5.5 - h))
    pos = jnp.arange(c_, dtype=F32)
    dist = pos[:, None] - pos[None, :]
    intra_f = jnp.where(dist >= 0, jnp.exp(lg_f[:, None, None] * jnp.maximum(dist, 0.0)), 0.0)
    intra_b = jnp.where(dist <= 0, jnp.exp(lg_b[:, None, None] * jnp.maximum(-dist, 0.0)), 0.0)
    qd_f = jnp.exp(lg_f[:, None] * (pos + 1.0))
    kd_f = jnp.exp(lg_f[:, None] * (c_ - 1.0 - pos))
    qd_b = jnp.exp(lg_b[:, None] * (c_ - pos))
    kd_b = jnp.exp(lg_b[:, None] * pos)
    cd_f = jnp.broadcast_to(jnp.exp(lg_f * c_)[:, None], (h_ret, c_))
    cd_b = jnp.broadcast_to(jnp.exp(lg_b * c_)[:, None], (h_ret, c_))
    zero = jnp.zeros((h_ret, c_), F32)
    cols = jnp.stack([qd_f, kd_f, qd_b, kd_b, cd_f, cd_b, zero, zero], axis=2)
    return intra_f, intra_b, cols


def kernel(x_prompt, x_sample, c, cache_k, cache_v, state_ret, c_ctx, w_mod, b_mod, w_in, w_out,
           lam_q1, lam_k1, lam_q2, lam_k2, subln_g, ln_g, ln_b, w_router, b_router, w1, w3, w2):
    batch, seq, d = x_prompt.shape
    dec_batch, dec_seq, _ = x_sample.shape
    depth = w_mod.shape[0]
    h_diff = cache_k.shape[2]
    h_ret = state_ret.shape[3]
    n_exp = w1.shape[1]
    assert cache_k.shape[-1] == cache_v.shape[-1] == HEAD_W
    assert state_ret.shape[-2:] == (HEAD_W, HEAD_W)
    alpha = (2 * depth) ** 0.25

    n_cond = 8
    tok = _Tokens(batch, seq, dec_batch, dec_seq, n_cond)
    cs = jnp.concatenate([c_ctx[None], c, jnp.zeros((n_cond - 1 - dec_batch, d), F32)], 0)
    mod6 = _modulation(cs, w_mod, b_mod).reshape(depth * n_cond * 6, 1, d)

    w_in_b = w_in.astype(BF16)
    w_out_b = w_out.astype(BF16)
    w1b, w3b, w2b = w1.astype(BF16), w3.astype(BF16), w2.astype(BF16)
    ck_b = cache_k.astype(BF16)
    cv_b = jnp.concatenate([cache_v.astype(BF16), jnp.ones(cache_v.shape, BF16)], axis=-1)
    lamp = jnp.stack([lam_q1, lam_k1, lam_q2, lam_k2], axis=1)
    subln = subln_g.reshape(depth, 1, HEAD_W)
    ln_g4 = ln_g.reshape(depth, 2, 1, d)
    ln_b4 = ln_b.reshape(depth, 2, 1, d)
    w_router_hi = w_router.T.astype(BF16)
    w_router_t = jnp.stack([w_router_hi, (w_router.T - w_router_hi.astype(F32)).astype(BF16)])
    b_router_t = b_router.reshape(n_exp, 1)
    rope = _rope_tables(dec_seq)
    ret_tabs = _ret_tables(h_ret)

    x = jnp.concatenate([x_prompt.reshape(tok.t_p, d), x_sample.reshape(tok.t_s, d)], 0)
    h = _premod(x, mod6, tok)
    ks_out, vs_out, ss_out = [], [], []
    for layer in range(depth):
        lam_init = 0.8 - 0.6 * math.exp(-0.3 * layer)
        z = _in_proj(h, w_in_b, layer)
        od_p, own_k, own_v = _attn_prompt(z, lamp, subln, layer, lam_init, batch, seq, h_diff)
        qkv = _rope_prep(z, rope, tok, h_diff)
        od_s = _attn_sample(qkv, ck_b, cv_b, lamp, subln, layer, lam_init, dec_batch, dec_seq, h_diff)
        or_p, s_new = _retention(z, ret_tabs, None, layer, batch, seq, 0, h_diff, h_ret, True)
        (or_s,) = _retention(z, ret_tabs, state_ret, layer, dec_batch, dec_seq, tok.t_p, h_diff, h_ret, False)
        ks_out.append(own_k)
        vs_out.append(own_v)
        ss_out.append(s_new)
        o_d = jnp.concatenate([od_p, od_s], 0)
        o_r = jnp.concatenate([or_p, or_s], 0)
        x1, h2, idx, gate = _out_proj(o_d, o_r, w_out_b, x, mod6, ln_g4, ln_b4, w_router_t, b_router_t,
                                      layer, tok, alpha)
        plan = _dispatch_plan(idx, gate, n_exp, tok.t)
        y2 = _experts(h2, *plan, w1b, w3b, w2b, layer, TOP_K * tok.t)
        last = layer == depth - 1
        outs = _ln2(x1, y2, mod6, ln_g4, ln_b4, layer, tok, alpha, last)
        x = outs[0]
        if not last:
            h = outs[1]
    xp = x[:tok.t_p].reshape(batch, seq, d)
    xs = x[tok.t_p:].reshape(dec_batch, dec_seq, d)
    return (xp, xs, jnp.stack(ks_out, axis=1), jnp.stack(vs_out, axis=1), jnp.stack(ss_out, axis=1))
```

```python
import functools
import math

import jax
import jax.numpy as jnp
from jax import lax
from jax.experimental import pallas as pl
from jax.experimental.pallas import tpu as pltpu

F32 = jnp.float32
BF16 = jnp.bfloat16

GRID_W = 64
ROPE_BASE = 10000.0
RET_CHUNK = 128
N_GROUPS = 4
TOP_K = 2
LN_EPS = 1e-5
HEAD_W = 128
MOE_ROWS = 256
VMEM_LIMIT = 56 * 1024 * 1024

_NT = (((1,), (1,)), ((), ()))
Q_SCALE = 0.125 * math.log2(math.e)


def _tile(n, pref, mult=8):
    t = min(n, pref)
    while n % t or (t % mult and t != n):
        t -= 1
    return t


def _params(sem, vmem=VMEM_LIMIT, flags=None):
    return pltpu.CompilerParams(dimension_semantics=sem, vmem_limit_bytes=vmem, flags=flags)


def _mod_kernel(c_ref, w_ref, b_ref, o_ref):
    c = c_ref[...]
    s = c * jax.nn.sigmoid(c)
    o_ref[...] = jnp.dot(s, w_ref[...], precision=lax.Precision.HIGHEST,
                         preferred_element_type=F32) + b_ref[...]


def _modulation(cs, w_mod, b_mod):
    depth, d, n = w_mod.shape
    rows = cs.shape[0]
    tn = _tile(n, 1024, 128)
    return pl.pallas_call(
        _mod_kernel,
        grid=(depth, n // tn),
        in_specs=[pl.BlockSpec((rows, d), lambda l, j: (0, 0)),
                  pl.BlockSpec((None, d, tn), lambda l, j: (l, 0, j)),
                  pl.BlockSpec((None, 1, tn), lambda l, j: (l, 0, j))],
        out_specs=pl.BlockSpec((None, rows, tn), lambda l, j: (l, 0, j)),
        out_shape=jax.ShapeDtypeStruct((depth, rows, n), F32),
        compiler_params=_params(("parallel", "parallel")),
        name="adaln_mod",
    )(cs, w_mod, b_mod.reshape(depth, 1, n))


class _Tokens:
    def __init__(self, batch, seq, dec_batch, dec_seq, n_cond):
        self.t_p = batch * seq
        self.t_s = dec_batch * dec_seq
        self.t = self.t_p + self.t_s
        self.dec_seq = dec_seq
        self.n_cond = n_cond

    def tile(self, pref):
        return _tile(math.gcd(self.t_p, self.dec_seq), pref)

    def group_spec(self, tm, w, prompt):
        n_p = self.t_p // tm
        if prompt:
            return pl.BlockSpec((tm, w), lambda i: (jnp.minimum(i, n_p - 1), 0))
        return pl.BlockSpec((tm, w), lambda i: (jnp.maximum(i - n_p, 0), 0))

    def mod_spec(self, layer, which, tm, d):
        t_p, dec_seq, n_cond = self.t_p, self.dec_seq, self.n_cond

        def idx(i, *_):
            r = i * tm
            cond = jnp.where(r < t_p, 0, 1 + (r - t_p) // dec_seq)
            return ((layer * n_cond + cond) * 6 + which, 0, 0)

        return pl.BlockSpec((None, 1, d), idx)


def _vec_spec(d):
    return pl.BlockSpec((1, d), lambda i, *_: (0, 0))


def _premod_kernel(xp_ref, xs_ref, sh_ref, sc_ref, x_ref, h_ref, *, n_p):
    x = jnp.where(pl.program_id(0) < n_p, xp_ref[...], xs_ref[...])
    x_ref[...] = x
    h_ref[...] = (x * (1.0 + sc_ref[...]) + sh_ref[...]).astype(BF16)


def _premod(x_p, x_s, mod6, tok):
    d = x_p.shape[1]
    tm = tok.tile(512)
    n_p = tok.t_p // tm
    row = pl.BlockSpec((tm, d), lambda i: (i, 0))
    return pl.pallas_call(
        functools.partial(_premod_kernel, n_p=n_p),
        grid=(tok.t // tm,),
        in_specs=[tok.group_spec(tm, d, True), tok.group_spec(tm, d, False),
                  tok.mod_spec(0, 0, tm, d), tok.mod_spec(0, 1, tm, d)],
        out_specs=[row, row],
        out_shape=[jax.ShapeDtypeStruct((tok.t, d), F32), jax.ShapeDtypeStruct((tok.t, d), BF16)],
        compiler_params=_params(("parallel",)),
        name="premod",
    )(x_p, x_s, mod6, mod6)


def _matmul_kernel(a_ref, w_ref, o_ref):
    o_ref[...] = jnp.dot(a_ref[...], w_ref[...], preferred_element_type=F32)


def _in_proj(h, w_in_b, layer):
    t, d = h.shape
    n = w_in_b.shape[-1]
    tm = _tile(t, 1024)
    tn = _tile(n, 1024, 128)
    return pl.pallas_call(
        _matmul_kernel,
        grid=(t // tm, n // tn),
        in_specs=[pl.BlockSpec((tm, d), lambda i, j: (i, 0)),
                  pl.BlockSpec((None, d, tn), lambda i, j: (layer, 0, j))],
        out_specs=pl.BlockSpec((tm, tn), lambda i, j: (i, j)),
        out_shape=jax.ShapeDtypeStruct((t, n), F32),
        compiler_params=_params(("parallel", "parallel")),
        name="in_proj",
    )(h, w_in_b)


def _prep_kernel(z_ref, cos_ref, sa_ref, sb_ref, o_ref, *, n_rot, n_q):
    cos, sa, sb = cos_ref[...], sa_ref[...], sb_ref[...]
    for j in range(n_rot):
        x = z_ref[:, j * HEAD_W:(j + 1) * HEAD_W]
        r = x * cos + pltpu.roll(x, HEAD_W - 16, 1) * sa + pltpu.roll(x, 16, 1) * sb
        if j < n_q:
            r = r * Q_SCALE
        o_ref[:, j * HEAD_W:(j + 1) * HEAD_W] = r.astype(BF16)
    ones = jnp.ones((z_ref.shape[0], HEAD_W), BF16)
    for j in range(n_q):
        c0 = (n_rot + 2 * j) * HEAD_W
        o_ref[:, c0:c0 + HEAD_W] = z_ref[:, (n_rot + j) * HEAD_W:(n_rot + j + 1) * HEAD_W].astype(BF16)
        o_ref[:, c0 + HEAD_W:c0 + 2 * HEAD_W] = ones


def _rope_prep(z, tables, tok, h_diff):
    cos, sa, sb = tables
    w = 3 * h_diff * HEAD_W
    wo = 4 * h_diff * HEAD_W
    tm = _tile(tok.dec_seq, 256)
    off = tok.t_p // tm
    nseq = tok.dec_seq // tm
    tab = pl.BlockSpec((tm, HEAD_W), lambda i: (i % nseq, 0))
    return pl.pallas_call(
        functools.partial(_prep_kernel, n_rot=2 * h_diff, n_q=h_diff),
        grid=(tok.t_s // tm,),
        in_specs=[pl.BlockSpec((tm, w), lambda i: (off + i, 0)), tab, tab, tab],
        out_specs=pl.BlockSpec((tm, wo), lambda i: (i, 0)),
        out_shape=jax.ShapeDtypeStruct((tok.t_s, wo), BF16),
        compiler_params=_params(("parallel",)),
        name="rope_prep",
    )(z, cos, sa, sb)


def _lam_value(lamp_ref, lam_init):
    lp = lamp_ref[...]
    a = jnp.sum(lp[0:1] * lp[1:2], axis=-1, keepdims=True)
    b = jnp.sum(lp[2:3] * lp[3:4], axis=-1, keepdims=True)
    return jnp.exp(a) - jnp.exp(b) + lam_init


def _softmax_pv(qc, chunks):
    m = acc = None
    for kk, vv in chunks:
        kk, vv = kk(), vv()
        s = lax.dot_general(qc, kk, _NT, preferred_element_type=F32)
        cm = jnp.max(s, axis=-1, keepdims=True)
        m_new = cm if m is None else jnp.maximum(m, cm)
        pv = jnp.dot(jnp.exp2(s - m_new).astype(BF16), vv, preferred_element_type=F32)
        acc = pv if m is None else jnp.exp2(m - m_new) * acc + pv
        m = m_new
    return acc[:, :HEAD_W] / acc[:, HEAD_W:]


def _q_halves(q):
    lane = lax.broadcasted_iota(jnp.int32, q.shape, 1)
    zero = jnp.zeros_like(q)
    return jnp.where(lane < HEAD_W // 2, q, zero), jnp.where(lane >= HEAD_W // 2, q, zero)


def _diff_norm(o1, o2, lam, g, lam_init):
    o = o1 - lam * o2
    o = o * lax.rsqrt(jnp.mean(o * o, axis=-1, keepdims=True) + LN_EPS) * g
    return o * (1.0 - lam_init)


def _diff_out(q, chunks, lam, g, lam_init):
    q1, q2 = _q_halves(q)
    return _diff_norm(_softmax_pv(q1, chunks), _softmax_pv(q2, chunks), lam, g, lam_init)


def _half_sq_norms(x):
    xf = x.astype(F32)
    sq = xf * xf
    lane = lax.broadcasted_iota(jnp.int32, sq.shape, 1)
    lo = jnp.sum(jnp.where(lane < HEAD_W // 2, sq, 0.0), axis=-1, keepdims=True)
    return lo, jnp.sum(sq, axis=-1, keepdims=True) - lo


def _softmax_pv_bounded(qc, bound, chunks):
    acc = None
    for kk, vv in chunks:
        kk, vv = kk(), vv()
        s = lax.dot_general(qc, kk, _NT, preferred_element_type=F32)
        pv = jnp.dot(jnp.exp2(s - bound).astype(BF16), vv, preferred_element_type=F32)
        acc = pv if acc is None else acc + pv
    return acc[:, :HEAD_W], acc[:, HEAD_W:]


def _attn_prompt_kernel(lamp_ref, q_ref, k_ref, v_ref, g_ref, o_ref, ok_ref, ov_ref, *, lam_init):
    k = k_ref[...]
    v = v_ref[...]
    ok_ref[...] = k
    ov_ref[...] = v
    q = (q_ref[...] * Q_SCALE).astype(BF16)
    lam = _lam_value(lamp_ref, lam_init)
    v1 = jnp.concatenate([v.astype(BF16), jnp.ones(v.shape, BF16)], axis=1)
    kb = k.astype(BF16)
    o = _diff_out(q, [(lambda: kb, lambda: v1)], lam, g_ref[...], lam_init)
    o_ref[...] = o.astype(BF16)


def _attn_prompt(z, lamp, subln_g, layer, lam_init, batch, seq, h_diff):
    t_p = batch * seq
    blk = lambda c0: pl.BlockSpec((seq, HEAD_W), lambda b, h: (b, c0 + h))
    own = pl.BlockSpec((None, None, seq, HEAD_W), lambda b, h: (b, h, 0, 0))
    return pl.pallas_call(
        functools.partial(_attn_prompt_kernel, lam_init=lam_init),
        grid=(batch, h_diff),
        in_specs=[pl.BlockSpec((None, 4, HEAD_W // 2), lambda b, h: (layer, 0, 0)),
                  blk(0), blk(h_diff), blk(2 * h_diff),
                  pl.BlockSpec((None, 1, HEAD_W), lambda b, h: (layer, 0, 0))],
        out_specs=[pl.BlockSpec((seq, HEAD_W), lambda b, h: (b, h)), own, own],
        out_shape=[jax.ShapeDtypeStruct((t_p, h_diff * HEAD_W), BF16),
                   jax.ShapeDtypeStruct((batch, h_diff, seq, HEAD_W), F32),
                   jax.ShapeDtypeStruct((batch, h_diff, seq, HEAD_W), F32)],
        compiler_params=_params(("parallel", "parallel")),
        name="diff_attn_prompt",
    )(lamp, z, z, z, subln_g)


_MIN_ROW_SUM = 2.0 ** -64


def _attn_sample_kernel(lamp_ref, q_ref, k_ref, v_ref, ck_ref, cv_ref, g_ref, o_ref, kmax_ref, *, lam_init, kc):
    n = k_ref.shape[0]
    chunks = [(lambda: ck_ref[...], lambda: cv_ref[...])]
    for c in range(n // kc):
        rows = slice(c * kc, (c + 1) * kc)
        chunks.append((lambda rows=rows: k_ref[rows, :], lambda rows=rows: v_ref[rows, :]))
    lam = _lam_value(lamp_ref, lam_init)
    g = g_ref[...]
    q = q_ref[...]

    @pl.when(pl.program_id(2) == 0)
    def _():
        lo_c, hi_c = _half_sq_norms(ck_ref[...])
        lo_k, hi_k = _half_sq_norms(k_ref[...])
        lo = jnp.maximum(jnp.max(lo_c, axis=0, keepdims=True), jnp.max(lo_k, axis=0, keepdims=True))
        hi = jnp.maximum(jnp.max(hi_c, axis=0, keepdims=True), jnp.max(hi_k, axis=0, keepdims=True))
        kmax_ref[0] = jnp.broadcast_to(jnp.sqrt(lo), kmax_ref.shape[1:])
        kmax_ref[1] = jnp.broadcast_to(jnp.sqrt(hi), kmax_ref.shape[1:])

    q1, q2 = _q_halves(q)
    qlo, qhi = _half_sq_norms(q)
    a1, l1 = _softmax_pv_bounded(q1, jnp.sqrt(qlo) * kmax_ref[0, 0:1, 0:1], chunks)
    a2, l2 = _softmax_pv_bounded(q2, jnp.sqrt(qhi) * kmax_ref[1, 0:1, 0:1], chunks)
    o_ref[...] = _diff_norm(a1 / l1, a2 / l2, lam, g, lam_init).astype(BF16)

    @pl.when(jnp.logical_not(jnp.min(jnp.minimum(l1, l2)) >= _MIN_ROW_SUM))
    def _():
        o_ref[...] = _diff_out(q, chunks, lam, g, lam_init).astype(BF16)


def _attn_sample(qkv, ck, cv, lamp, subln_g, layer, lam_init, dec_batch, dec_seq, h_diff):
    tq = _tile(dec_seq, 512)
    kc = _tile(dec_seq, 512)
    nq = dec_seq // tq
    past = ck.shape[3]
    ctx = lambda w: pl.BlockSpec((None, None, None, past, w), lambda b, h, i: (b, layer, h, 0, 0))
    return pl.pallas_call(
        functools.partial(_attn_sample_kernel, lam_init=lam_init, kc=kc),
        grid=(dec_batch, h_diff, nq),
        in_specs=[pl.BlockSpec((None, 4, HEAD_W // 2), lambda b, h, i: (layer, 0, 0)),
                  pl.BlockSpec((tq, HEAD_W), lambda b, h, i: (b * nq + i, h)),
                  pl.BlockSpec((dec_seq, HEAD_W), lambda b, h, i: (b, h_diff + h)),
                  pl.BlockSpec((dec_seq, 2 * HEAD_W), lambda b, h, i: (b, h_diff + h)),
                  ctx(HEAD_W), ctx(2 * HEAD_W),
                  pl.BlockSpec((None, 1, HEAD_W), lambda b, h, i: (layer, 0, 0))],
        out_specs=pl.BlockSpec((tq, HEAD_W), lambda b, h, i: (b * nq + i, h)),
        out_shape=jax.ShapeDtypeStruct((dec_batch * dec_seq, h_diff * HEAD_W), BF16),
        scratch_shapes=[pltpu.VMEM((2, 8, HEAD_W), F32)],
        compiler_params=_params(("parallel", "parallel", "arbitrary")),
        name="diff_attn_sample",
    )(lamp, qkv, qkv, qkv, ck, cv, subln_g)


def _ret_kernel(*refs, has_state, emit_state, nc, scale):
    q_ref, k_ref, v_ref, g_ref, if_ref, ib_ref, cols_ref = refs[:7]
    refs = refs[7:]
    if has_state:
        s0_ref, refs = refs[0], refs[1:]
    o_ref, refs = refs[0], refs[1:]
    if emit_state:
        s_ref, refs = refs[0], refs[1:]
    of_ref, ob_ref, sf_ref, sb_ref = refs
    c_ = RET_CHUNK
    intra_f = if_ref[...]
    intra_b = ib_ref[...]
    cols = cols_ref[...]
    qdf, kdf, qdb, kdb = cols[:, 0:1], cols[:, 1:2], cols[:, 2:3], cols[:, 3:4]
    cdf, cdb = cols[0:1, 4:5], cols[0:1, 5:6]

    def chunk_rows(c):
        return pl.ds(pl.multiple_of(c * c_, c_), c_)

    def intra_and_state(c, s, intra, kd, cd, out_ref, st_ref):
        rows = chunk_rows(c)
        st_ref[c] = s.astype(BF16)
        q = q_ref[rows, :].astype(BF16)
        k = k_ref[rows, :] * scale
        v = v_ref[rows, :].astype(BF16)
        a = lax.dot_general(q, k.astype(BF16), _NT, preferred_element_type=F32) * intra
        out_ref[rows, :] = jnp.dot(a.astype(BF16), v, preferred_element_type=F32)
        kt = jnp.transpose(k * kd).astype(BF16)
        return cd * s + jnp.dot(kt, v, preferred_element_type=F32)

    def scan(i, carry):
        sf, sb = carry
        sf = intra_and_state(i, sf, intra_f, kdf, cdf, of_ref, sf_ref)
        sb = intra_and_state(nc - 1 - i, sb, intra_b, kdb, cdb, ob_ref, sb_ref)
        return sf, sb

    if has_state:
        init = (s0_ref[0], s0_ref[1])
    else:
        init = (jnp.zeros((HEAD_W, HEAD_W), F32), jnp.zeros((HEAD_W, HEAD_W), F32))
    unroll = _tile(nc, 4, 1)
    sf, sb = lax.fori_loop(0, nc, scan, init, unroll=unroll)
    if emit_state:
        s_ref[0] = sf
        s_ref[1] = sb

    def finish(c, carry):
        rows = chunk_rows(c)
        q = q_ref[rows, :]
        o = of_ref[rows, :] + ob_ref[rows, :]
        o = o + jnp.dot((q * qdf).astype(BF16), sf_ref[c], preferred_element_type=F32)
        o = o + jnp.dot((q * qdb).astype(BF16), sb_ref[c], preferred_element_type=F32)
        mu = jnp.mean(o, axis=-1, keepdims=True)
        oc = o - mu
        o = oc * lax.rsqrt(jnp.mean(oc * oc, axis=-1, keepdims=True) + LN_EPS)
        g = g_ref[rows, :]
        o_ref[rows, :] = (o * (g * jax.nn.sigmoid(g))).astype(BF16)
        return carry

    lax.fori_loop(0, nc, finish, 0, unroll=unroll)


def _retention(z, tabs, state, layer, n_seq, seq, row0, h_diff, h_ret, emit_state):
    intra_f, intra_b, cols = tabs
    c_ = RET_CHUNK
    nc = seq // c_
    assert row0 % seq == 0
    off = row0 // seq
    c0 = 3 * h_diff
    blk = lambda g: pl.BlockSpec((seq, HEAD_W), lambda b, h: (off + b, c0 + g * h_ret + h))
    htab = lambda shape: pl.BlockSpec((None,) + shape, lambda b, h: (h,) + (0,) * len(shape))
    in_specs = [blk(0), blk(1), blk(2), blk(3), htab((c_, c_)), htab((c_, c_)), htab((c_, 8))]
    args = [z, z, z, z, intra_f, intra_b, cols]
    if state is not None:
        in_specs.append(pl.BlockSpec((None, None, 2, None, HEAD_W, HEAD_W),
                                     lambda b, h: (b, layer, 0, h, 0, 0)))
        args.append(state)
    out_specs = [pl.BlockSpec((seq, HEAD_W), lambda b, h: (b, h))]
    out_shape = [jax.ShapeDtypeStruct((n_seq * seq, h_ret * HEAD_W), BF16)]
    if emit_state:
        out_specs.append(pl.BlockSpec((None, 2, None, HEAD_W, HEAD_W), lambda b, h: (b, 0, h, 0, 0)))
        out_shape.append(jax.ShapeDtypeStruct((n_seq, 2, h_ret, HEAD_W, HEAD_W), F32))

    return pl.pallas_call(
        functools.partial(_ret_kernel, has_state=state is not None, emit_state=emit_state, nc=nc,
                          scale=HEAD_W ** -0.5),
        grid=(n_seq, h_ret),
        in_specs=in_specs,
        out_specs=out_specs,
        out_shape=out_shape,
        scratch_shapes=[pltpu.VMEM((seq, HEAD_W), F32), pltpu.VMEM((seq, HEAD_W), F32),
                        pltpu.VMEM((nc, HEAD_W, HEAD_W), BF16), pltpu.VMEM((nc, HEAD_W, HEAD_W), BF16)],
        compiler_params=_params(("parallel", "parallel")),
        name="retention_%d" % seq,
    )(*args)


def _layer_norm(v, g, b):
    mu = jnp.mean(v, axis=-1, keepdims=True)
    vc = v - mu
    var = jnp.mean(vc * vc, axis=-1, keepdims=True)
    return vc * lax.rsqrt(var + LN_EPS) * g + b


def _route(p, n_exp):
    per = n_exp // N_GROUPS
    rows = [p[e:e + 1, :] for e in range(n_exp)]
    scores = []
    for g in range(N_GROUPS):
        a, b, c, d = rows[per * g:per * g + per]
        hi1, lo1, hi2, lo2 = jnp.maximum(a, b), jnp.minimum(a, b), jnp.maximum(c, d), jnp.minimum(c, d)
        scores.append(jnp.maximum(hi1, hi2) + jnp.maximum(jnp.minimum(hi1, hi2), jnp.maximum(lo1, lo2)))
    best = scores[0]
    gsel = jnp.zeros(best.shape, jnp.int32)
    for g in range(1, N_GROUPS):
        better = scores[g] > best
        best = jnp.where(better, scores[g], best)
        gsel = jnp.where(better, g, gsel)
    vals = []
    for j in range(per):
        v = rows[(N_GROUPS - 1) * per + j]
        for g in range(N_GROUPS - 2, -1, -1):
            v = jnp.where(gsel == g, rows[g * per + j], v)
        vals.append(v)
    b1, i1 = vals[0], jnp.zeros(best.shape, jnp.int32)
    for j in range(1, per):
        gt = vals[j] > b1
        b1 = jnp.where(gt, vals[j], b1)
        i1 = jnp.where(gt, j, i1)
    b2, i2 = jnp.full(best.shape, -1.0, F32), jnp.zeros(best.shape, jnp.int32)
    for j in range(per):
        gt = jnp.where(i1 == j, -2.0, vals[j]) > b2
        b2 = jnp.where(gt, vals[j], b2)
        i2 = jnp.where(gt, j, i2)
    tot = b1 + b2
    return gsel * per + i1, gsel * per + i2, b1 / tot, b2 / tot


def _outproj_kernel(odp_ref, ods_ref, orp_ref, ors_ref, wd_ref, wr_ref, x_ref, ga_ref, shf_ref, scf_ref,
                    lng_ref, lnb_ref, wrt_ref, brt_ref, x1_ref, h2_ref, idx_ref, gate_ref, *, alpha, n_p):
    prompt = pl.program_id(0) < n_p
    o_d = jnp.where(prompt, odp_ref[...], ods_ref[...])
    o_r = jnp.where(prompt, orp_ref[...], ors_ref[...])
    mix = jnp.dot(o_d, wd_ref[...], preferred_element_type=F32)
    mix = mix + jnp.dot(o_r, wr_ref[...], preferred_element_type=F32)
    x1 = _layer_norm(alpha * x_ref[...] + ga_ref[...] * mix, lng_ref[...], lnb_ref[...])
    x1_ref[...] = x1
    h2 = x1 * (1.0 + scf_ref[...]) + shf_ref[...]
    h2_ref[...] = h2
    h_hi = h2.astype(BF16)
    h_lo = (h2 - h_hi.astype(F32)).astype(BF16)
    w_hi, w_lo = wrt_ref[0], wrt_ref[1]
    logits = (lax.dot_general(w_hi, h_hi, _NT, preferred_element_type=F32)
              + lax.dot_general(w_lo, h_hi, _NT, preferred_element_type=F32)
              + lax.dot_general(w_hi, h_lo, _NT, preferred_element_type=F32)) + brt_ref[...]
    e = jnp.exp(logits - jnp.max(logits, axis=0, keepdims=True))
    p = e / jnp.sum(e, axis=0, keepdims=True)
    e1, e2, g1, g2 = _route(p, logits.shape[0])
    idx_ref[0:1, :] = e1
    idx_ref[1:2, :] = e2
    gate_ref[0:1, :] = g1
    gate_ref[1:2, :] = g2


def _out_proj(od_p, od_s, or_p, or_s, w_out_b, x, mod6, ln_g, ln_b, w_router_t, b_router_t, layer, tok, alpha):
    t, d = x.shape
    wd = od_p.shape[1]
    wr = or_p.shape[1]
    n_exp = w_router_t.shape[1]
    tm = tok.tile(512)
    row = lambda w: pl.BlockSpec((tm, w), lambda i: (i, 0))
    lnv = pl.BlockSpec((None, None, 1, d), lambda i: (layer, 0, 0, 0))
    return pl.pallas_call(
        functools.partial(_outproj_kernel, alpha=alpha, n_p=tok.t_p // tm),
        grid=(t // tm,),
        in_specs=[tok.group_spec(tm, wd, True), tok.group_spec(tm, wd, False),
                  tok.group_spec(tm, wr, True), tok.group_spec(tm, wr, False),
                  pl.BlockSpec((None, wd, d), lambda i: (layer, 0, 0)),
                  pl.BlockSpec((None, wr, d), lambda i: (layer, wd // wr, 0)),
                  row(d),
                  tok.mod_spec(layer, 2, tm, d), tok.mod_spec(layer, 3, tm, d), tok.mod_spec(layer, 4, tm, d),
                  lnv, lnv,
                  pl.BlockSpec((2, n_exp, d), lambda i: (0, 0, 0)),
                  pl.BlockSpec((n_exp, 1), lambda i: (0, 0))],
        out_specs=[row(d), row(d),
                   pl.BlockSpec((TOP_K, tm), lambda i: (0, i)),
                   pl.BlockSpec((TOP_K, tm), lambda i: (0, i))],
        out_shape=[jax.ShapeDtypeStruct((t, d), F32), jax.ShapeDtypeStruct((t, d), F32),
                   jax.ShapeDtypeStruct((TOP_K, t), jnp.int32), jax.ShapeDtypeStruct((TOP_K, t), F32)],
        compiler_params=_params(("parallel",)),
        name="out_proj_ln_router",
    )(od_p, od_s, or_p, or_s, w_out_b, w_out_b, x, mod6, mod6, mod6, ln_g, ln_b, w_router_t, b_router_t)


def _moe_kernel(be_ref, cnt_ref, tok0_ref, tokn_ref, dst_ref, gate_ref, x_hbm, w1_ref, w3_ref, w2_ref, y_hbm,
                xbuf, ybuf, gsem, ssem, *, n_out):
    del be_ref
    s = pl.program_id(0)
    rows = xbuf.shape[1]
    c_old, c_cur, c_next = cnt_ref[s], cnt_ref[s + 2], cnt_ref[s + 3]

    def gather_copy(tok_ref, r, p):
        return pltpu.make_async_copy(x_hbm.at[pl.ds(tok_ref[0, 0, r], 1), :], xbuf.at[p, pl.ds(r, 1), :], gsem.at[p])

    def scatter_copy(r, p):
        return pltpu.make_async_copy(ybuf.at[p, pl.ds(r, 1), :], y_hbm.at[pl.ds(dst_ref[0, 0, r], 1), :], ssem.at[p])

    @pl.when(s == 0)
    def _():
        for p in range(2):
            ybuf[p] = jnp.zeros(ybuf.shape[1:], F32)
            spare = pltpu.make_async_copy(ybuf.at[p], y_hbm.at[pl.ds(n_out + p * rows, rows), :], ssem.at[p])
            spare.start()
            spare.wait()

        def first(r, c):
            gather_copy(tok0_ref, r, 0).start()
            return c

        lax.fori_loop(0, rows, first, 0)

    for p in range(2):
        mine = s % 2 == p

        @pl.when(jnp.logical_and(mine, c_old > 0))
        def _():
            pltpu.make_async_copy(ybuf.at[p], y_hbm.at[pl.ds(0, rows), :], ssem.at[p]).wait()

        @pl.when(jnp.logical_and(mine, c_next > 0))
        def _():
            for r in range(rows):
                gather_copy(tokn_ref, r, 1 - p).start()

        @pl.when(jnp.logical_and(mine, c_cur > 0))
        def _():
            pltpu.make_async_copy(x_hbm.at[pl.ds(0, rows), :], xbuf.at[p], gsem.at[p]).wait()
            x = xbuf[p].astype(BF16)
            a = jnp.dot(x, w1_ref[...], preferred_element_type=F32)
            g = jnp.dot(x, w3_ref[...], preferred_element_type=F32)
            h = (a * jax.nn.sigmoid(a) * g).astype(BF16)
            ybuf[p] = jnp.dot(h, w2_ref[...], preferred_element_type=F32) * gate_ref[...]
            for r in range(rows):
                scatter_copy(r, p).start()


def _experts(h2, block_e, block_cnt, slot_tok, slot_dst, slot_gate, w1b, w3b, w2b, layer, n_out):
    t, d = h2.shape
    nb = block_e.shape[0]
    ff = w1b.shape[-1]
    rows = MOE_ROWS
    last = nb - 1
    cnt_pad = jnp.concatenate([jnp.zeros((2,), jnp.int32), block_cnt, jnp.zeros((3,), jnp.int32)])
    smem = lambda off: pl.BlockSpec((1, 1, rows), lambda s, be, n: (jnp.minimum(s + off, last), 0, 0),
                                    memory_space=pltpu.SMEM)
    wspec = lambda shape: pl.BlockSpec((None, None) + shape,
                                       lambda s, be, n: (layer, be[jnp.minimum(s, last)], 0, 0))
    grid_spec = pltpu.PrefetchScalarGridSpec(
        num_scalar_prefetch=2,
        grid=(nb + 2,),
        in_specs=[smem(0), smem(1), smem(0),
                  pl.BlockSpec((rows, 1), lambda s, be, n: (jnp.minimum(s, last), 0)),
                  pl.BlockSpec(memory_space=pl.ANY),
                  wspec((d, ff)), wspec((d, ff)), wspec((ff, d))],
        out_specs=pl.BlockSpec(memory_space=pl.ANY),
        scratch_shapes=[pltpu.VMEM((2, rows, d), F32), pltpu.VMEM((2, rows, d), F32),
                        pltpu.SemaphoreType.DMA((2,)), pltpu.SemaphoreType.DMA((2,))])
    tok3 = slot_tok.reshape(nb, 1, rows)
    return pl.pallas_call(
        functools.partial(_moe_kernel, n_out=n_out),
        grid_spec=grid_spec,
        out_shape=jax.ShapeDtypeStruct((n_out + 2 * rows, d), F32),
        compiler_params=_params(("arbitrary",)),
        name="moe_experts",
    )(block_e, cnt_pad, tok3, tok3, slot_dst.reshape(nb, 1, rows), slot_gate.reshape(nb * rows, 1),
      h2, w1b, w3b, w2b)


def _dispatch_plan(idx, gate, n_exp, t):
    rows = MOE_ROWS
    a = t * TOP_K
    flat_e = idx.T.reshape(a)
    flat_g = gate.T.reshape(a)
    _, order = lax.sort((flat_e, jnp.arange(a, dtype=jnp.int32)), num_keys=1, is_stable=True)
    experts = jnp.arange(n_exp, dtype=jnp.int32)
    counts = jnp.sum((flat_e[None, :] == experts[:, None]).astype(jnp.int32), axis=1)
    padded = (counts + rows - 1) // rows * rows
    ends_p = jnp.cumsum(padded)
    starts_p = ends_p - padded
    starts = jnp.cumsum(counts) - counts
    nb = -(-a // rows) + n_exp
    blk = jnp.arange(nb, dtype=jnp.int32)
    block_e = jnp.minimum(jnp.sum((blk[:, None] * rows >= ends_p[None, :]).astype(jnp.int32), axis=1), n_exp - 1)
    onehot = block_e[:, None] == experts[None, :]
    pick = lambda tab: jnp.sum(jnp.where(onehot, tab[None, :], 0), axis=1)
    within = (blk * rows - pick(starts_p))[:, None] + jnp.arange(rows, dtype=jnp.int32)[None, :]
    valid = within < pick(counts)[:, None]
    src = order[jnp.clip(pick(starts)[:, None] + within, 0, a - 1)]
    slot_tok = jnp.where(valid, src // TOP_K, 0)
    spare = a + (blk % 2)[:, None] * rows + jnp.arange(rows, dtype=jnp.int32)[None, :]
    slot_dst = jnp.where(valid, (src % TOP_K) * t + src // TOP_K, spare)
    slot_gate = jnp.where(valid, flat_g[src], 0.0)
    block_cnt = jnp.sum(valid.astype(jnp.int32), axis=1)
    return block_e, block_cnt, slot_tok, slot_dst, slot_gate


def _ln2_kernel(*refs, alpha, emit_h, n_p):
    x1_ref, y0_ref, y1_ref, gf_ref, lng_ref, lnb_ref = refs[:6]
    y = y0_ref[...] + y1_ref[...]
    x2 = _layer_norm(alpha * x1_ref[...] + gf_ref[...] * y, lng_ref[...], lnb_ref[...])
    if emit_h:
        sh_ref, sc_ref, x2_ref, h_ref = refs[6:]
        h_ref[...] = (x2 * (1.0 + sc_ref[...]) + sh_ref[...]).astype(BF16)
        x2_ref[...] = x2
    else:
        xp_ref, xs_ref = refs[6:]

        @pl.when(pl.program_id(0) < n_p)
        def _():
            xp_ref[...] = x2

        @pl.when(pl.program_id(0) >= n_p)
        def _():
            xs_ref[...] = x2


def _ln2(x1, y2, mod6, ln_g, ln_b, layer, tok, alpha, last):
    t, d = x1.shape
    tm = tok.tile(512)
    nt = t // tm
    row = lambda off: pl.BlockSpec((tm, d), lambda i: (off + i, 0))
    lnv = pl.BlockSpec((None, None, 1, d), lambda i: (layer, 1, 0, 0))
    in_specs = [row(0), row(0), row(nt), tok.mod_spec(layer, 5, tm, d), lnv, lnv]
    args = [x1, y2, y2, mod6, ln_g, ln_b]
    if last:
        out_specs = [tok.group_spec(tm, d, True), tok.group_spec(tm, d, False)]
        out_shape = [jax.ShapeDtypeStruct((tok.t_p, d), F32), jax.ShapeDtypeStruct((tok.t_s, d), F32)]
    else:
        in_specs += [tok.mod_spec(layer + 1, 0, tm, d), tok.mod_spec(layer + 1, 1, tm, d)]
        args += [mod6, mod6]
        out_specs = [row(0), row(0)]
        out_shape = [jax.ShapeDtypeStruct((t, d), F32), jax.ShapeDtypeStruct((t, d), BF16)]
    return pl.pallas_call(
        functools.partial(_ln2_kernel, alpha=alpha, emit_h=not last, n_p=tok.t_p // tm),
        grid=(nt,),
        in_specs=in_specs,
        out_specs=out_specs,
        out_shape=out_shape,
        compiler_params=_params(("arbitrary",)),
        name="ln2_mod",
    )(*args)


def _rope_tables(n_tok):
    rows = n_tok // GRID_W
    axis = HEAD_W // 4
    r = jnp.repeat(jnp.arange(rows, dtype=F32), GRID_W)
    col = jnp.tile(jnp.arange(GRID_W, dtype=F32), rows)
    inv = ROPE_BASE ** (-jnp.arange(0, axis, 2, dtype=F32) / axis)
    ar = r[:, None] * inv[None]
    ac = col[:, None] * inv[None]
    ang = jnp.concatenate([ar, ar, ac, ac], -1)
    cos, sin = jnp.cos(ang), jnp.sin(ang)
    cos, sin = jnp.tile(cos, (1, 2)), jnp.tile(sin, (1, 2))
    low = (jnp.arange(HEAD_W) % (axis)) < axis // 2
    return cos, jnp.where(low, -sin, 0.0), jnp.where(low, 0.0, sin)


def _ret_tables(h_ret):
    c_ = RET_CHUNK
    h = jnp.arange(h_ret, dtype=F32)
    lg_f = jnp.log(1.0 - jnp.exp2(-5.0 - h))
    lg_b = jnp.log(1.0 - jnp.exp2(-5.5 - h))
    pos = jnp.arange(c_, dtype=F32)
    dist = pos[:, None] - pos[None, :]
    intra_f = jnp.where(dist >= 0, jnp.exp(lg_f[:, None, None] * jnp.maximum(dist, 0.0)), 0.0)
    intra_b = jnp.where(dist <= 0, jnp.exp(lg_b[:, None, None] * jnp.maximum(-dist, 0.0)), 0.0)
    qd_f = jnp.exp(lg_f[:, None] * (pos + 1.0))
    kd_f = jnp.exp(lg_f[:, None] * (c_ - 1.0 - pos))
    qd_b = jnp.exp(lg_b[:, None] * (c_ - pos))
    kd_b = jnp.exp(lg_b[:, None] * pos)
    cd_f = jnp.broadcast_to(jnp.exp(lg_f * c_)[:, None], (h_ret, c_))
    cd_b = jnp.broadcast_to(jnp.exp(lg_b * c_)[:, None], (h_ret, c_))
    zero = jnp.zeros((h_ret, c_), F32)
    cols = jnp.stack([qd_f, kd_f, qd_b, kd_b, cd_f, cd_b, zero, zero], axis=2)
    return intra_f, intra_b, cols


def kernel(x_prompt, x_sample, c, cache_k, cache_v, state_ret, c_ctx, w_mod, b_mod, w_in, w_out,
           lam_q1, lam_k1, lam_q2, lam_k2, subln_g, ln_g, ln_b, w_router, b_router, w1, w3, w2):
    batch, seq, d = x_prompt.shape
    dec_batch, dec_seq, _ = x_sample.shape
    depth = w_mod.shape[0]
    h_diff = cache_k.shape[2]
    h_ret = state_ret.shape[3]
    n_exp = w1.shape[1]
    assert cache_k.shape[-1] == cache_v.shape[-1] == HEAD_W
    assert state_ret.shape[-2:] == (HEAD_W, HEAD_W)
    alpha = (2 * depth) ** 0.25

    n_cond = 8
    tok = _Tokens(batch, seq, dec_batch, dec_seq, n_cond)
    cs = jnp.concatenate([c_ctx[None], c, jnp.zeros((n_cond - 1 - dec_batch, d), F32)], 0)
    mod6 = _modulation(cs, w_mod, b_mod).reshape(depth * n_cond * 6, 1, d)

    w_in_b = w_in.astype(BF16)
    w_out_b = w_out.astype(BF16)
    w1b, w3b, w2b = w1.astype(BF16), w3.astype(BF16), w2.astype(BF16)
    ck_b = cache_k.astype(BF16)
    cv_b = jnp.concatenate([cache_v.astype(BF16), jnp.ones(cache_v.shape, BF16)], axis=-1)
    lamp = jnp.stack([lam_q1, lam_k1, lam_q2, lam_k2], axis=1)
    subln = subln_g.reshape(depth, 1, HEAD_W)
    ln_g4 = ln_g.reshape(depth, 2, 1, d)
    ln_b4 = ln_b.reshape(depth, 2, 1, d)
    w_router_hi = w_router.T.astype(BF16)
    w_router_t = jnp.stack([w_router_hi, (w_router.T - w_router_hi.astype(F32)).astype(BF16)])
    b_router_t = b_router.reshape(n_exp, 1)
    rope = _rope_tables(dec_seq)
    ret_tabs = _ret_tables(h_ret)

    x, h = _premod(x_prompt.reshape(tok.t_p, d), x_sample.reshape(tok.t_s, d), mod6, tok)
    ks_out, vs_out, ss_out = [], [], []
    for layer in range(depth):
        lam_init = 0.8 - 0.6 * math.exp(-0.3 * layer)
        z = _in_proj(h, w_in_b, layer)
        od_p, own_k, own_v = _attn_prompt(z, lamp, subln, layer, lam_init, batch, seq, h_diff)
        qkv = _rope_prep(z, rope, tok, h_diff)
        od_s = _attn_sample(qkv, ck_b, cv_b, lamp, subln, layer, lam_init, dec_batch, dec_seq, h_diff)
        or_p, s_new = _retention(z, ret_tabs, None, layer, batch, seq, 0, h_diff, h_ret, True)
        (or_s,) = _retention(z, ret_tabs, state_ret, layer, dec_batch, dec_seq, tok.t_p, h_diff, h_ret, False)
        ks_out.append(own_k)
        vs_out.append(own_v)
        ss_out.append(s_new)
        x1, h2, idx, gate = _out_proj(od_p, od_s, or_p, or_s, w_out_b, x, mod6, ln_g4, ln_b4,
                                      w_router_t, b_router_t, layer, tok, alpha)
        plan = _dispatch_plan(idx, gate, n_exp, tok.t)
        y2 = _experts(h2, *plan, w1b, w3b, w2b, layer, TOP_K * tok.t)
        last = layer == depth - 1
        outs = _ln2(x1, y2, mod6, ln_g4, ln_b4, layer, tok, alpha, last)
        x, h = outs
    xp = x.reshape(batch, seq, d)
    xs = h.reshape(dec_batch, dec_seq, d)
    return (xp, xs, jnp.stack(ks_out, axis=1), jnp.stack(vs_out, axis=1), jnp.stack(ss_out, axis=1))
```

```python
import functools
import math

import jax
import jax.numpy as jnp
from jax import lax
from jax.experimental import pallas as pl
from jax.experimental.pallas import tpu as pltpu

F32 = jnp.float32
BF16 = jnp.bfloat16

GRID_W = 64
ROPE_BASE = 10000.0
RET_CHUNK = 128
N_GROUPS = 4
TOP_K = 2
LN_EPS = 1e-5
HEAD_W = 128
MOE_ROWS = 256
VMEM_LIMIT = 56 * 1024 * 1024

_NT = (((1,), (1,)), ((), ()))
Q_SCALE = 0.125 * math.log2(math.e)


def _tile(n, pref, mult=8):
    t = min(n, pref)
    while n % t or (t % mult and t != n):
        t -= 1
    return t


def _params(sem, vmem=VMEM_LIMIT, flags=None):
    return pltpu.CompilerParams(dimension_semantics=sem, vmem_limit_bytes=vmem, flags=flags)


def _mod_kernel(c_ref, w_ref, b_ref, o_ref):
    c = c_ref[...]
    s = c * jax.nn.sigmoid(c)
    o_ref[...] = jnp.dot(s, w_ref[...], precision=lax.Precision.HIGHEST,
                         preferred_element_type=F32) + b_ref[...]


def _modulation(cs, w_mod, b_mod):
    depth, d, n = w_mod.shape
    rows = cs.shape[0]
    tn = _tile(n, 1024, 128)
    return pl.pallas_call(
        _mod_kernel,
        grid=(depth, n // tn),
        in_specs=[pl.BlockSpec((rows, d), lambda l, j: (0, 0)),
                  pl.BlockSpec((None, d, tn), lambda l, j: (l, 0, j)),
                  pl.BlockSpec((None, 1, tn), lambda l, j: (l, 0, j))],
        out_specs=pl.BlockSpec((None, rows, tn), lambda l, j: (l, 0, j)),
        out_shape=jax.ShapeDtypeStruct((depth, rows, n), F32),
        compiler_params=_params(("parallel", "parallel")),
        name="adaln_mod",
    )(cs, w_mod, b_mod.reshape(depth, 1, n))


class _Tokens:
    def __init__(self, batch, seq, dec_batch, dec_seq, n_cond):
        self.t_p = batch * seq
        self.t_s = dec_batch * dec_seq
        self.t = self.t_p + self.t_s
        self.dec_seq = dec_seq
        self.n_cond = n_cond

    def tile(self, pref):
        return _tile(math.gcd(self.t_p, self.dec_seq), pref)

    def group_spec(self, tm, w, prompt):
        n_p = self.t_p // tm
        n_s = self.t_s // tm
        if prompt:
            return pl.BlockSpec((tm, w), lambda i: (jnp.minimum(i, n_p - 1), 0))
        return pl.BlockSpec((tm, w), lambda i: (jnp.clip(i - n_p, 0, n_s - 1), 0))

    def mod_spec(self, layer, which, tm, d):
        t_p, dec_seq, n_cond = self.t_p, self.dec_seq, self.n_cond

        def idx(i, *_):
            r = i * tm
            cond = jnp.where(r < t_p, 0, 1 + (r - t_p) // dec_seq)
            return ((layer * n_cond + cond) * 6 + which, 0, 0)

        return pl.BlockSpec((None, 1, d), idx)


def _vec_spec(d):
    return pl.BlockSpec((1, d), lambda i, *_: (0, 0))


def _premod_kernel(xp_ref, xs_ref, sh_ref, sc_ref, x_ref, h_ref, *, n_p):
    x = jnp.where(pl.program_id(0) < n_p, xp_ref[...], xs_ref[...])
    x_ref[...] = x
    h_ref[...] = (x * (1.0 + sc_ref[...]) + sh_ref[...]).astype(BF16)


def _premod(x_p, x_s, mod6, tok):
    d = x_p.shape[1]
    tm = tok.tile(512)
    n_p = tok.t_p // tm
    row = pl.BlockSpec((tm, d), lambda i: (i, 0))
    return pl.pallas_call(
        functools.partial(_premod_kernel, n_p=n_p),
        grid=(tok.t // tm,),
        in_specs=[tok.group_spec(tm, d, True), tok.group_spec(tm, d, False),
                  tok.mod_spec(0, 0, tm, d), tok.mod_spec(0, 1, tm, d)],
        out_specs=[row, row],
        out_shape=[jax.ShapeDtypeStruct((tok.t, d), F32), jax.ShapeDtypeStruct((tok.t, d), BF16)],
        compiler_params=_params(("parallel",)),
        name="premod",
    )(x_p, x_s, mod6, mod6)


def _matmul_kernel(a_ref, w_ref, o_ref):
    o_ref[...] = jnp.dot(a_ref[...], w_ref[...], preferred_element_type=F32)


def _in_proj(h, w_in_b, layer):
    t, d = h.shape
    n = w_in_b.shape[-1]
    tm = _tile(t, 1024)
    tn = _tile(n, 1024, 128)
    return pl.pallas_call(
        _matmul_kernel,
        grid=(t // tm, n // tn),
        in_specs=[pl.BlockSpec((tm, d), lambda i, j: (i, 0)),
                  pl.BlockSpec((None, d, tn), lambda i, j: (layer, 0, j))],
        out_specs=pl.BlockSpec((tm, tn), lambda i, j: (i, j)),
        out_shape=jax.ShapeDtypeStruct((t, n), F32),
        compiler_params=_params(("parallel", "parallel")),
        name="in_proj",
    )(h, w_in_b)


def _rope(x, cos, sa, sb):
    return x * cos + pltpu.roll(x, HEAD_W - 16, 1) * sa + pltpu.roll(x, 16, 1) * sb


def _lam_value(lamp_ref, lam_init):
    lp = lamp_ref[...]
    a = jnp.sum(lp[0:1] * lp[1:2], axis=-1, keepdims=True)
    b = jnp.sum(lp[2:3] * lp[3:4], axis=-1, keepdims=True)
    return jnp.exp(a) - jnp.exp(b) + lam_init


def _softmax_pv(qc, chunks):
    m = acc = None
    for kk, vv in chunks:
        kk, vv = kk(), vv()
        s = lax.dot_general(qc, kk, _NT, preferred_element_type=F32)
        cm = jnp.max(s, axis=-1, keepdims=True)
        m_new = cm if m is None else jnp.maximum(m, cm)
        pv = jnp.dot(jnp.exp2(s - m_new).astype(BF16), vv, preferred_element_type=F32)
        acc = pv if m is None else jnp.exp2(m - m_new) * acc + pv
        m = m_new
    return acc[:, :HEAD_W] / acc[:, HEAD_W:]


def _q_halves(q):
    lane = lax.broadcasted_iota(jnp.int32, q.shape, 1)
    zero = jnp.zeros_like(q)
    return jnp.where(lane < HEAD_W // 2, q, zero), jnp.where(lane >= HEAD_W // 2, q, zero)


def _diff_norm(o1, o2, lam, g, lam_init):
    o = o1 - lam * o2
    o = o * lax.rsqrt(jnp.mean(o * o, axis=-1, keepdims=True) + LN_EPS) * g
    return o * (1.0 - lam_init)


def _diff_out(q, chunks, lam, g, lam_init):
    q1, q2 = _q_halves(q)
    return _diff_norm(_softmax_pv(q1, chunks), _softmax_pv(q2, chunks), lam, g, lam_init)


def _half_sq_norms(x):
    xf = x.astype(F32)
    sq = xf * xf
    lane = lax.broadcasted_iota(jnp.int32, sq.shape, 1)
    lo = jnp.sum(jnp.where(lane < HEAD_W // 2, sq, 0.0), axis=-1, keepdims=True)
    return lo, jnp.sum(sq, axis=-1, keepdims=True) - lo


def _softmax_pv_bounded(qc, bound, key_chunks, p_ref, vall_ref):
    col = 0
    for kk in key_chunks:
        kk = kk()
        s = lax.dot_general(qc, kk, _NT, preferred_element_type=F32)
        p_ref[:, col:col + kk.shape[0]] = jnp.exp2(s - bound).astype(BF16)
        col += kk.shape[0]
    acc = jnp.dot(p_ref[...], vall_ref[...], preferred_element_type=F32)
    return acc[:, :HEAD_W], acc[:, HEAD_W:]


def _attn_prompt_kernel(lamp_ref, q_ref, k_ref, v_ref, g_ref, o_ref, ok_ref, ov_ref, *, lam_init, hp):
    lam = _lam_value(lamp_ref, lam_init)
    g = g_ref[...]
    for j in range(hp):
        cols = slice(j * HEAD_W, (j + 1) * HEAD_W)
        k = k_ref[:, cols]
        v = v_ref[:, cols]
        ok_ref[j] = k
        ov_ref[j] = v
        q = (q_ref[:, cols] * Q_SCALE).astype(BF16)
        v1 = jnp.concatenate([v.astype(BF16), jnp.ones(v.shape, BF16)], axis=1)
        kb = k.astype(BF16)
        o = _diff_out(q, [(lambda kb=kb: kb, lambda v1=v1: v1)], lam, g, lam_init)
        o_ref[:, cols] = o.astype(BF16)


def _attn_prompt(z, lamp, subln_g, layer, lam_init, batch, seq, h_diff):
    t_p = batch * seq
    hp = _tile(h_diff, 4, 1)
    w = hp * HEAD_W
    ng = h_diff // hp
    blk = lambda c0: pl.BlockSpec((seq, w), lambda b, g: (b, c0 * ng + g))
    own = pl.BlockSpec((None, hp, seq, HEAD_W), lambda b, g: (b, g, 0, 0))
    return pl.pallas_call(
        functools.partial(_attn_prompt_kernel, lam_init=lam_init, hp=hp),
        grid=(batch, ng),
        in_specs=[pl.BlockSpec((None, 4, HEAD_W // 2), lambda b, g: (layer, 0, 0)),
                  blk(0), blk(1), blk(2),
                  pl.BlockSpec((None, 1, HEAD_W), lambda b, g: (layer, 0, 0))],
        out_specs=[pl.BlockSpec((seq, w), lambda b, g: (b, g)), own, own],
        out_shape=[jax.ShapeDtypeStruct((t_p, h_diff * HEAD_W), BF16),
                   jax.ShapeDtypeStruct((batch, h_diff, seq, HEAD_W), F32),
                   jax.ShapeDtypeStruct((batch, h_diff, seq, HEAD_W), F32)],
        compiler_params=_params(("parallel", "parallel")),
        name="diff_attn_prompt",
    )(lamp, z, z, z, subln_g)


_MIN_ROW_SUM = 2.0 ** -64


def _attn_sample_kernel(lamp_ref, q_ref, k_ref, v_ref, ck_ref, cv_ref, g_ref, cosq_ref, saq_ref, sbq_ref,
                        cos_ref, sa_ref, sb_ref, o_ref, kmax_ref, kr_ref, vall_ref, p1_ref, p2_ref,
                        *, lam_init, kc):
    n = k_ref.shape[0]
    past = ck_ref.shape[0]

    @pl.when(pl.program_id(2) == 0)
    def _():
        kr_ref[...] = _rope(k_ref[...], cos_ref[...], sa_ref[...], sb_ref[...]).astype(BF16)
        vall_ref[:past, :] = cv_ref[...]
        vall_ref[past:, :HEAD_W] = v_ref[...].astype(BF16)
        vall_ref[past:, HEAD_W:] = jnp.ones((n, HEAD_W), BF16)
        lo_c, hi_c = _half_sq_norms(ck_ref[...])
        lo_k, hi_k = _half_sq_norms(kr_ref[...])
        lo = jnp.maximum(jnp.max(lo_c, axis=0, keepdims=True), jnp.max(lo_k, axis=0, keepdims=True))
        hi = jnp.maximum(jnp.max(hi_c, axis=0, keepdims=True), jnp.max(hi_k, axis=0, keepdims=True))
        kmax_ref[0] = jnp.broadcast_to(jnp.sqrt(lo), kmax_ref.shape[1:])
        kmax_ref[1] = jnp.broadcast_to(jnp.sqrt(hi), kmax_ref.shape[1:])

    keys = [lambda: ck_ref[...]]
    vals = [lambda: vall_ref[:past, :]]
    for c in range(n // kc):
        keys.append(lambda c=c: kr_ref[c * kc:(c + 1) * kc, :])
        vals.append(lambda c=c: vall_ref[past + c * kc:past + (c + 1) * kc, :])
    lam = _lam_value(lamp_ref, lam_init)
    g = g_ref[...]
    q = (_rope(q_ref[...], cosq_ref[...], saq_ref[...], sbq_ref[...]) * Q_SCALE).astype(BF16)

    q1, q2 = _q_halves(q)
    qlo, qhi = _half_sq_norms(q)
    a1, l1 = _softmax_pv_bounded(q1, jnp.sqrt(qlo) * kmax_ref[0, 0:1, 0:1], keys, p1_ref, vall_ref)
    a2, l2 = _softmax_pv_bounded(q2, jnp.sqrt(qhi) * kmax_ref[1, 0:1, 0:1], keys, p2_ref, vall_ref)
    o_ref[...] = _diff_norm(a1 / l1, a2 / l2, lam, g, lam_init).astype(BF16)

    @pl.when(jnp.logical_not(jnp.min(jnp.minimum(l1, l2)) >= _MIN_ROW_SUM))
    def _():
        o_ref[...] = _diff_out(q, list(zip(keys, vals)), lam, g, lam_init).astype(BF16)


def _attn_sample(z, row0, tables, ck, cv, lamp, subln_g, layer, lam_init, dec_batch, dec_seq, h_diff):
    assert row0 % dec_seq == 0
    tq = _tile(dec_seq, 512)
    kc = _tile(dec_seq, 512)
    nq = dec_seq // tq
    s0 = row0 // dec_seq
    past = ck.shape[3]
    ctx = lambda w: pl.BlockSpec((None, None, None, past, w), lambda b, h, i: (b, layer, h, 0, 0))
    own = lambda g: pl.BlockSpec((dec_seq, HEAD_W), lambda b, h, i: (s0 + b, g * h_diff + h))
    tab_q = pl.BlockSpec((tq, HEAD_W), lambda b, h, i: (i, 0))
    tab_k = pl.BlockSpec((dec_seq, HEAD_W), lambda b, h, i: (0, 0))
    return pl.pallas_call(
        functools.partial(_attn_sample_kernel, lam_init=lam_init, kc=kc),
        grid=(dec_batch, h_diff, nq),
        in_specs=[pl.BlockSpec((None, 4, HEAD_W // 2), lambda b, h, i: (layer, 0, 0)),
                  pl.BlockSpec((tq, HEAD_W), lambda b, h, i: ((s0 + b) * nq + i, h)),
                  own(1), own(2), ctx(HEAD_W), ctx(2 * HEAD_W),
                  pl.BlockSpec((None, 1, HEAD_W), lambda b, h, i: (layer, 0, 0)),
                  tab_q, tab_q, tab_q, tab_k, tab_k, tab_k],
        out_specs=pl.BlockSpec((tq, HEAD_W), lambda b, h, i: (b * nq + i, h)),
        out_shape=jax.ShapeDtypeStruct((dec_batch * dec_seq, h_diff * HEAD_W), BF16),
        scratch_shapes=[pltpu.VMEM((2, 8, HEAD_W), F32), pltpu.VMEM((dec_seq, HEAD_W), BF16),
                        pltpu.VMEM((past + dec_seq, 2 * HEAD_W), BF16),
                        pltpu.VMEM((tq, past + dec_seq), BF16), pltpu.VMEM((tq, past + dec_seq), BF16)],
        compiler_params=_params(("parallel", "parallel", "arbitrary")),
        name="diff_attn_sample",
    )(lamp, z, z, z, ck, cv, subln_g, *tables, *tables)


def _ret_kernel(*refs, has_state, emit_state, nc, scale, hp):
    q_ref, k_ref, v_ref, g_ref, if_ref, ib_ref, cols_ref = refs[:7]
    refs = refs[7:]
    if has_state:
        s0_ref, refs = refs[0], refs[1:]
    o_ref, refs = refs[0], refs[1:]
    if emit_state:
        s_ref, refs = refs[0], refs[1:]
    of_ref, ob_ref, sf_ref, sb_ref = refs
    c_ = RET_CHUNK
    unroll = _tile(nc, 4, 1)

    def chunk_rows(c):
        return pl.ds(pl.multiple_of(c * c_, c_), c_)

    for j in range(hp):
        lanes = slice(j * HEAD_W, (j + 1) * HEAD_W)
        intra_f = if_ref[j]
        intra_b = ib_ref[j]
        cols = cols_ref[j]
        qdf, kdf, qdb, kdb = cols[:, 0:1], cols[:, 1:2], cols[:, 2:3], cols[:, 3:4]
        cdf, cdb = cols[0:1, 4:5], cols[0:1, 5:6]

        def intra_and_state(c, s, intra, kd, cd, out_ref, st_ref):
            rows = chunk_rows(c)
            st_ref[c] = s.astype(BF16)
            q = q_ref[rows, lanes].astype(BF16)
            k = k_ref[rows, lanes] * scale
            v = v_ref[rows, lanes].astype(BF16)
            a = lax.dot_general(q, k.astype(BF16), _NT, preferred_element_type=F32) * intra
            out_ref[rows, :] = jnp.dot(a.astype(BF16), v, preferred_element_type=F32)
            kt = jnp.transpose(k * kd).astype(BF16)
            return cd * s + jnp.dot(kt, v, preferred_element_type=F32)

        def scan(i, carry):
            sf, sb = carry
            sf = intra_and_state(i, sf, intra_f, kdf, cdf, of_ref, sf_ref)
            sb = intra_and_state(nc - 1 - i, sb, intra_b, kdb, cdb, ob_ref, sb_ref)
            return sf, sb

        if has_state:
            init = (s0_ref[0, j], s0_ref[1, j])
        else:
            init = (jnp.zeros((HEAD_W, HEAD_W), F32), jnp.zeros((HEAD_W, HEAD_W), F32))
        sf, sb = lax.fori_loop(0, nc, scan, init, unroll=unroll)
        if emit_state:
            s_ref[0, j] = sf
            s_ref[1, j] = sb

        def finish(c, carry):
            rows = chunk_rows(c)
            q = q_ref[rows, lanes]
            o = of_ref[rows, :] + ob_ref[rows, :]
            o = o + jnp.dot((q * qdf).astype(BF16), sf_ref[c], preferred_element_type=F32)
            o = o + jnp.dot((q * qdb).astype(BF16), sb_ref[c], preferred_element_type=F32)
            mu = jnp.mean(o, axis=-1, keepdims=True)
            oc = o - mu
            o = oc * lax.rsqrt(jnp.mean(oc * oc, axis=-1, keepdims=True) + LN_EPS)
            g = g_ref[rows, lanes]
            o_ref[rows, lanes] = (o * (g * jax.nn.sigmoid(g))).astype(BF16)
            return carry

        lax.fori_loop(0, nc, finish, 0, unroll=unroll)


def _retention(z, tabs, state, layer, n_seq, seq, row0, h_diff, h_ret, emit_state):
    intra_f, intra_b, cols = tabs
    c_ = RET_CHUNK
    nc = seq // c_
    assert row0 % seq == 0
    off = row0 // seq
    hp = _tile(h_ret, 4, 1) if nc <= 4 else 1
    w = hp * HEAD_W
    ng = h_ret // hp
    assert (3 * h_diff) % hp == 0
    c0 = 3 * h_diff // hp
    blk = lambda q: pl.BlockSpec((seq, w), lambda b, g: (off + b, c0 + q * ng + g))
    htab = lambda shape: pl.BlockSpec((hp,) + shape, lambda b, g: (g,) + (0,) * len(shape))
    in_specs = [blk(0), blk(1), blk(2), blk(3), htab((c_, c_)), htab((c_, c_)), htab((c_, 8))]
    args = [z, z, z, z, intra_f, intra_b, cols]
    if state is not None:
        in_specs.append(pl.BlockSpec((None, None, 2, hp, HEAD_W, HEAD_W), lambda b, g: (b, layer, 0, g, 0, 0)))
        args.append(state)
    out_specs = [pl.BlockSpec((seq, w), lambda b, g: (b, g))]
    out_shape = [jax.ShapeDtypeStruct((n_seq * seq, h_ret * HEAD_W), BF16)]
    if emit_state:
        out_specs.append(pl.BlockSpec((None, 2, hp, HEAD_W, HEAD_W), lambda b, g: (b, 0, g, 0, 0)))
        out_shape.append(jax.ShapeDtypeStruct((n_seq, 2, h_ret, HEAD_W, HEAD_W), F32))

    return pl.pallas_call(
        functools.partial(_ret_kernel, has_state=state is not None, emit_state=emit_state, nc=nc,
                          scale=HEAD_W ** -0.5, hp=hp),
        grid=(n_seq, ng),
        in_specs=in_specs,
        out_specs=out_specs,
        out_shape=out_shape,
        scratch_shapes=[pltpu.VMEM((seq, HEAD_W), F32), pltpu.VMEM((seq, HEAD_W), F32),
                        pltpu.VMEM((nc, HEAD_W, HEAD_W), BF16), pltpu.VMEM((nc, HEAD_W, HEAD_W), BF16)],
        compiler_params=_params(("parallel", "parallel")),
        name="retention_%d" % seq,
    )(*args)


def _layer_norm(v, g, b):
    mu = jnp.mean(v, axis=-1, keepdims=True)
    vc = v - mu
    var = jnp.mean(vc * vc, axis=-1, keepdims=True)
    return vc * lax.rsqrt(var + LN_EPS) * g + b


def _route(p, n_exp):
    per = n_exp // N_GROUPS
    rows = [p[e:e + 1, :] for e in range(n_exp)]
    scores = []
    for g in range(N_GROUPS):
        a, b, c, d = rows[per * g:per * g + per]
        hi1, lo1, hi2, lo2 = jnp.maximum(a, b), jnp.minimum(a, b), jnp.maximum(c, d), jnp.minimum(c, d)
        scores.append(jnp.maximum(hi1, hi2) + jnp.maximum(jnp.minimum(hi1, hi2), jnp.maximum(lo1, lo2)))
    best = scores[0]
    gsel = jnp.zeros(best.shape, jnp.int32)
    for g in range(1, N_GROUPS):
        better = scores[g] > best
        best = jnp.where(better, scores[g], best)
        gsel = jnp.where(better, g, gsel)
    vals = []
    for j in range(per):
        v = rows[(N_GROUPS - 1) * per + j]
        for g in range(N_GROUPS - 2, -1, -1):
            v = jnp.where(gsel == g, rows[g * per + j], v)
        vals.append(v)
    b1, i1 = vals[0], jnp.zeros(best.shape, jnp.int32)
    for j in range(1, per):
        gt = vals[j] > b1
        b1 = jnp.where(gt, vals[j], b1)
        i1 = jnp.where(gt, j, i1)
    b2, i2 = jnp.full(best.shape, -1.0, F32), jnp.zeros(best.shape, jnp.int32)
    for j in range(per):
        gt = jnp.where(i1 == j, -2.0, vals[j]) > b2
        b2 = jnp.where(gt, vals[j], b2)
        i2 = jnp.where(gt, j, i2)
    tot = b1 + b2
    return gsel * per + i1, gsel * per + i2, b1 / tot, b2 / tot


def _outproj_kernel(odp_ref, ods_ref, orp_ref, ors_ref, wd_ref, wr_ref, x_ref, ga_ref, shf_ref, scf_ref,
                    lng_ref, lnb_ref, wrt_ref, brt_ref, x1_ref, h2_ref, idx_ref, gate_ref, *, alpha, n_p):
    prompt = pl.program_id(0) < n_p
    o_d = jnp.where(prompt, odp_ref[...], ods_ref[...])
    o_r = jnp.where(prompt, orp_ref[...], ors_ref[...])
    mix = jnp.dot(o_d, wd_ref[...], preferred_element_type=F32)
    mix = mix + jnp.dot(o_r, wr_ref[...], preferred_element_type=F32)
    x1 = _layer_norm(alpha * x_ref[...] + ga_ref[...] * mix, lng_ref[...], lnb_ref[...])
    x1_ref[...] = x1
    h2 = x1 * (1.0 + scf_ref[...]) + shf_ref[...]
    h2_ref[...] = h2
    h_hi = h2.astype(BF16)
    h_lo = (h2 - h_hi.astype(F32)).astype(BF16)
    w_hi, w_lo = wrt_ref[0], wrt_ref[1]
    logits = (lax.dot_general(w_hi, h_hi, _NT, preferred_element_type=F32)
              + lax.dot_general(w_lo, h_hi, _NT, preferred_element_type=F32)
              + lax.dot_general(w_hi, h_lo, _NT, preferred_element_type=F32)) + brt_ref[...]
    e = jnp.exp(logits - jnp.max(logits, axis=0, keepdims=True))
    p = e / jnp.sum(e, axis=0, keepdims=True)
    e1, e2, g1, g2 = _route(p, logits.shape[0])
    idx_ref[0:1, :] = e1
    idx_ref[1:2, :] = e2
    gate_ref[0:1, :] = g1
    gate_ref[1:2, :] = g2


def _out_proj(od_p, od_s, or_p, or_s, w_out_b, x, mod6, ln_g, ln_b, w_router_t, b_router_t, layer, tok, alpha):
    t, d = x.shape
    wd = od_p.shape[1]
    wr = or_p.shape[1]
    n_exp = w_router_t.shape[1]
    tm = tok.tile(512)
    row = lambda w: pl.BlockSpec((tm, w), lambda i: (i, 0))
    lnv = pl.BlockSpec((None, None, 1, d), lambda i: (layer, 0, 0, 0))
    return pl.pallas_call(
        functools.partial(_outproj_kernel, alpha=alpha, n_p=tok.t_p // tm),
        grid=(t // tm,),
        in_specs=[tok.group_spec(tm, wd, True), tok.group_spec(tm, wd, False),
                  tok.group_spec(tm, wr, True), tok.group_spec(tm, wr, False),
                  pl.BlockSpec((None, wd, d), lambda i: (layer, 0, 0)),
                  pl.BlockSpec((None, wr, d), lambda i: (layer, wd // wr, 0)),
                  row(d),
                  tok.mod_spec(layer, 2, tm, d), tok.mod_spec(layer, 3, tm, d), tok.mod_spec(layer, 4, tm, d),
                  lnv, lnv,
                  pl.BlockSpec((2, n_exp, d), lambda i: (0, 0, 0)),
                  pl.BlockSpec((n_exp, 1), lambda i: (0, 0))],
        out_specs=[row(d), row(d),
                   pl.BlockSpec((TOP_K, tm), lambda i: (0, i)),
                   pl.BlockSpec((TOP_K, tm), lambda i: (0, i))],
        out_shape=[jax.ShapeDtypeStruct((t, d), F32), jax.ShapeDtypeStruct((t, d), F32),
                   jax.ShapeDtypeStruct((TOP_K, t), jnp.int32), jax.ShapeDtypeStruct((TOP_K, t), F32)],
        compiler_params=_params(("parallel",)),
        name="out_proj_ln_router",
    )(od_p, od_s, or_p, or_s, w_out_b, w_out_b, x, mod6, mod6, mod6, ln_g, ln_b, w_router_t, b_router_t)


def _moe_kernel(be_ref, cnt_ref, tok0_ref, tokn_ref, dst_ref, gate_ref, x_hbm, w1_ref, w3_ref, w2_ref, y_hbm,
                xbuf, ybuf, gsem, ssem, *, n_out):
    del be_ref
    s = pl.program_id(0)
    rows = xbuf.shape[1]
    c_old, c_cur, c_next = cnt_ref[s], cnt_ref[s + 2], cnt_ref[s + 3]

    def gather_copy(tok_ref, r, p):
        return pltpu.make_async_copy(x_hbm.at[pl.ds(tok_ref[0, 0, r], 1), :], xbuf.at[p, pl.ds(r, 1), :], gsem.at[p])

    def scatter_copy(r, p):
        return pltpu.make_async_copy(ybuf.at[p, pl.ds(r, 1), :], y_hbm.at[pl.ds(dst_ref[0, 0, r], 1), :], ssem.at[p])

    @pl.when(s == 0)
    def _():
        for p in range(2):
            ybuf[p] = jnp.zeros(ybuf.shape[1:], F32)
            spare = pltpu.make_async_copy(ybuf.at[p], y_hbm.at[pl.ds(n_out + p * rows, rows), :], ssem.at[p])
            spare.start()
            spare.wait()

        def first(r, c):
            gather_copy(tok0_ref, r, 0).start()
            return c

        lax.fori_loop(0, rows, first, 0)

    for p in range(2):
        mine = s % 2 == p

        @pl.when(jnp.logical_and(mine, c_old > 0))
        def _():
            pltpu.make_async_copy(ybuf.at[p], y_hbm.at[pl.ds(0, rows), :], ssem.at[p]).wait()

        @pl.when(jnp.logical_and(mine, c_next > 0))
        def _():
            for r in range(rows):
                gather_copy(tokn_ref, r, 1 - p).start()

        @pl.when(jnp.logical_and(mine, c_cur > 0))
        def _():
            pltpu.make_async_copy(x_hbm.at[pl.ds(0, rows), :], xbuf.at[p], gsem.at[p]).wait()
            x = xbuf[p].astype(BF16)
            a = jnp.dot(x, w1_ref[...], preferred_element_type=F32)
            g = jnp.dot(x, w3_ref[...], preferred_element_type=F32)
            h = (a * jax.nn.sigmoid(a) * g).astype(BF16)
            ybuf[p] = jnp.dot(h, w2_ref[...], preferred_element_type=F32) * gate_ref[...]
            for r in range(rows):
                scatter_copy(r, p).start()


def _experts(h2, block_e, block_cnt, slot_tok, slot_dst, slot_gate, w1b, w3b, w2b, layer, n_out):
    t, d = h2.shape
    nb = block_e.shape[0]
    ff = w1b.shape[-1]
    rows = MOE_ROWS
    last = nb - 1
    cnt_pad = jnp.concatenate([jnp.zeros((2,), jnp.int32), block_cnt, jnp.zeros((3,), jnp.int32)])
    smem = lambda off: pl.BlockSpec((1, 1, rows), lambda s, be, n: (jnp.minimum(s + off, last), 0, 0),
                                    memory_space=pltpu.SMEM)
    wspec = lambda shape: pl.BlockSpec((None, None) + shape,
                                       lambda s, be, n: (layer, be[jnp.minimum(s, last)], 0, 0))
    grid_spec = pltpu.PrefetchScalarGridSpec(
        num_scalar_prefetch=2,
        grid=(nb + 2,),
        in_specs=[smem(0), smem(1), smem(0),
                  pl.BlockSpec((rows, 1), lambda s, be, n: (jnp.minimum(s, last), 0)),
                  pl.BlockSpec(memory_space=pl.ANY),
                  wspec((d, ff)), wspec((d, ff)), wspec((ff, d))],
        out_specs=pl.BlockSpec(memory_space=pl.ANY),
        scratch_shapes=[pltpu.VMEM((2, rows, d), F32), pltpu.VMEM((2, rows, d), F32),
                        pltpu.SemaphoreType.DMA((2,)), pltpu.SemaphoreType.DMA((2,))])
    tok3 = slot_tok.reshape(nb, 1, rows)
    return pl.pallas_call(
        functools.partial(_moe_kernel, n_out=n_out),
        grid_spec=grid_spec,
        out_shape=jax.ShapeDtypeStruct((n_out + 2 * rows, d), F32),
        compiler_params=_params(("arbitrary",)),
        name="moe_experts",
    )(block_e, cnt_pad, tok3, tok3, slot_dst.reshape(nb, 1, rows), slot_gate.reshape(nb * rows, 1),
      h2, w1b, w3b, w2b)


def _dispatch_plan(idx, gate, n_exp, t):
    rows = MOE_ROWS
    a = t * TOP_K
    flat_e = idx.T.reshape(a)
    flat_g = gate.T.reshape(a)
    _, order = lax.sort((flat_e, jnp.arange(a, dtype=jnp.int32)), num_keys=1, is_stable=True)
    experts = jnp.arange(n_exp, dtype=jnp.int32)
    counts = jnp.sum((flat_e[None, :] == experts[:, None]).astype(jnp.int32), axis=1)
    padded = (counts + rows - 1) // rows * rows
    ends_p = jnp.cumsum(padded)
    starts_p = ends_p - padded
    starts = jnp.cumsum(counts) - counts
    nb = -(-a // rows) + n_exp
    blk = jnp.arange(nb, dtype=jnp.int32)
    block_e = jnp.minimum(jnp.sum((blk[:, None] * rows >= ends_p[None, :]).astype(jnp.int32), axis=1), n_exp - 1)
    onehot = block_e[:, None] == experts[None, :]
    pick = lambda tab: jnp.sum(jnp.where(onehot, tab[None, :], 0), axis=1)
    within = (blk * rows - pick(starts_p))[:, None] + jnp.arange(rows, dtype=jnp.int32)[None, :]
    valid = within < pick(counts)[:, None]
    src = order[jnp.clip(pick(starts)[:, None] + within, 0, a - 1)]
    slot_tok = jnp.where(valid, src // TOP_K, 0)
    spare = a + (blk % 2)[:, None] * rows + jnp.arange(rows, dtype=jnp.int32)[None, :]
    slot_dst = jnp.where(valid, (src % TOP_K) * t + src // TOP_K, spare)
    slot_gate = jnp.where(valid, flat_g[src], 0.0)
    block_cnt = jnp.sum(valid.astype(jnp.int32), axis=1)
    return block_e, block_cnt, slot_tok, slot_dst, slot_gate


def _ln2_kernel(*refs, alpha, emit_h, n_p):
    x1_ref, y0_ref, y1_ref, gf_ref, lng_ref, lnb_ref = refs[:6]
    y = y0_ref[...] + y1_ref[...]
    x2 = _layer_norm(alpha * x1_ref[...] + gf_ref[...] * y, lng_ref[...], lnb_ref[...])
    if emit_h:
        sh_ref, sc_ref, x2_ref, h_ref = refs[6:]
        h_ref[...] = (x2 * (1.0 + sc_ref[...]) + sh_ref[...]).astype(BF16)
        x2_ref[...] = x2
    else:
        xp_ref, xs_ref = refs[6:]

        @pl.when(pl.program_id(0) < n_p)
        def _():
            xp_ref[...] = x2

        @pl.when(pl.program_id(0) >= n_p)
        def _():
            xs_ref[...] = x2


def _ln2(x1, y2, mod6, ln_g, ln_b, layer, tok, alpha, last):
    t, d = x1.shape
    tm = tok.tile(512)
    nt = t // tm
    row = lambda off: pl.BlockSpec((tm, d), lambda i: (off + i, 0))
    lnv = pl.BlockSpec((None, None, 1, d), lambda i: (layer, 1, 0, 0))
    in_specs = [row(0), row(0), row(nt), tok.mod_spec(layer, 5, tm, d), lnv, lnv]
    args = [x1, y2, y2, mod6, ln_g, ln_b]
    if last:
        out_specs = [tok.group_spec(tm, d, True), tok.group_spec(tm, d, False)]
        out_shape = [jax.ShapeDtypeStruct((tok.t_p, d), F32), jax.ShapeDtypeStruct((tok.t_s, d), F32)]
    else:
        in_specs += [tok.mod_spec(layer + 1, 0, tm, d), tok.mod_spec(layer + 1, 1, tm, d)]
        args += [mod6, mod6]
        out_specs = [row(0), row(0)]
        out_shape = [jax.ShapeDtypeStruct((t, d), F32), jax.ShapeDtypeStruct((t, d), BF16)]
    return pl.pallas_call(
        functools.partial(_ln2_kernel, alpha=alpha, emit_h=not last, n_p=tok.t_p // tm),
        grid=(nt,),
        in_specs=in_specs,
        out_specs=out_specs,
        out_shape=out_shape,
        compiler_params=_params(("arbitrary",)),
        name="ln2_mod",
    )(*args)


def _rope_tables(n_tok):
    rows = n_tok // GRID_W
    axis = HEAD_W // 4
    r = jnp.repeat(jnp.arange(rows, dtype=F32), GRID_W)
    col = jnp.tile(jnp.arange(GRID_W, dtype=F32), rows)
    inv = ROPE_BASE ** (-jnp.arange(0, axis, 2, dtype=F32) / axis)
    ar = r[:, None] * inv[None]
    ac = col[:, None] * inv[None]
    ang = jnp.concatenate([ar, ar, ac, ac], -1)
    cos, sin = jnp.cos(ang), jnp.sin(ang)
    cos, sin = jnp.tile(cos, (1, 2)), jnp.tile(sin, (1, 2))
    low = (jnp.arange(HEAD_W) % (axis)) < axis // 2
    return cos, jnp.where(low, -sin, 0.0), jnp.where(low, 0.0, sin)


def _ret_tables(h_ret):
    c_ = RET_CHUNK
    h = jnp.arange(h_ret, dtype=F32)
    lg_f = jnp.log(1.0 - jnp.exp2(-5.0 - h))
    lg_b = jnp.log(1.0 - jnp.exp2(-5.5 - h))
    pos = jnp.arange(c_, dtype=F32)
    dist = pos[:, None] - pos[None, :]
    intra_f = jnp.where(dist >= 0, jnp.exp(lg_f[:, None, None] * jnp.maximum(dist, 0.0)), 0.0)
    intra_b = jnp.where(dist <= 0, jnp.exp(lg_b[:, None, None] * jnp.maximum(-dist, 0.0)), 0.0)
    qd_f = jnp.exp(lg_f[:, None] * (pos + 1.0))
    kd_f = jnp.exp(lg_f[:, None] * (c_ - 1.0 - pos))
    qd_b = jnp.exp(lg_b[:, None] * (c_ - pos))
    kd_b = jnp.exp(lg_b[:, None] * pos)
    cd_f = jnp.broadcast_to(jnp.exp(lg_f * c_)[:, None], (h_ret, c_))
    cd_b = jnp.broadcast_to(jnp.exp(lg_b * c_)[:, None], (h_ret, c_))
    zero = jnp.zeros((h_ret, c_), F32)
    cols = jnp.stack([qd_f, kd_f, qd_b, kd_b, cd_f, cd_b, zero, zero], axis=2)
    return intra_f, intra_b, cols


def kernel(x_prompt, x_sample, c, cache_k, cache_v, state_ret, c_ctx, w_mod, b_mod, w_in, w_out,
           lam_q1, lam_k1, lam_q2, lam_k2, subln_g, ln_g, ln_b, w_router, b_router, w1, w3, w2):
    batch, seq, d = x_prompt.shape
    dec_batch, dec_seq, _ = x_sample.shape
    depth = w_mod.shape[0]
    h_diff = cache_k.shape[2]
    h_ret = state_ret.shape[3]
    n_exp = w1.shape[1]
    assert cache_k.shape[-1] == cache_v.shape[-1] == HEAD_W
    assert state_ret.shape[-2:] == (HEAD_W, HEAD_W)
    alpha = (2 * depth) ** 0.25

    n_cond = 8
    tok = _Tokens(batch, seq, dec_batch, dec_seq, n_cond)
    cs = jnp.concatenate([c_ctx[None], c, jnp.zeros((n_cond - 1 - dec_batch, d), F32)], 0)
    mod6 = _modulation(cs, w_mod, b_mod).reshape(depth * n_cond * 6, 1, d)

    w_in_b = w_in.astype(BF16)
    w_out_b = w_out.astype(BF16)
    w1b, w3b, w2b = w1.astype(BF16), w3.astype(BF16), w2.astype(BF16)
    ck_b = cache_k.astype(BF16)
    cv_b = jnp.concatenate([cache_v.astype(BF16), jnp.ones(cache_v.shape, BF16)], axis=-1)
    lamp = jnp.stack([lam_q1, lam_k1, lam_q2, lam_k2], axis=1)
    subln = subln_g.reshape(depth, 1, HEAD_W)
    ln_g4 = ln_g.reshape(depth, 2, 1, d)
    ln_b4 = ln_b.reshape(depth, 2, 1, d)
    w_router_hi = w_router.T.astype(BF16)
    w_router_t = jnp.stack([w_router_hi, (w_router.T - w_router_hi.astype(F32)).astype(BF16)])
    b_router_t = b_router.reshape(n_exp, 1)
    rope = _rope_tables(dec_seq)
    ret_tabs = _ret_tables(h_ret)

    x, h = _premod(x_prompt.reshape(tok.t_p, d), x_sample.reshape(tok.t_s, d), mod6, tok)
    ks_out, vs_out, ss_out = [], [], []
    for layer in range(depth):
        lam_init = 0.8 - 0.6 * math.exp(-0.3 * layer)
        z = _in_proj(h, w_in_b, layer)
        od_p, own_k, own_v = _attn_prompt(z, lamp, subln, layer, lam_init, batch, seq, h_diff)
        od_s = _attn_sample(z, tok.t_p, rope, ck_b, cv_b, lamp, subln, layer, lam_init, dec_batch, dec_seq, h_diff)
        or_p, s_new = _retention(z, ret_tabs, None, layer, batch, seq, 0, h_diff, h_ret, True)
        (or_s,) = _retention(z, ret_tabs, state_ret, layer, dec_batch, dec_seq, tok.t_p, h_diff, h_ret, False)
        ks_out.append(own_k)
        vs_out.append(own_v)
        ss_out.append(s_new)
        x1, h2, idx, gate = _out_proj(od_p, od_s, or_p, or_s, w_out_b, x, mod6, ln_g4, ln_b4,
                                      w_router_t, b_router_t, layer, tok, alpha)
        plan = _dispatch_plan(idx, gate, n_exp, tok.t)
        y2 = _experts(h2, *plan, w1b, w3b, w2b, layer, TOP_K * tok.t)
        last = layer == depth - 1
        outs = _ln2(x1, y2, mod6, ln_g4, ln_b4, layer, tok, alpha, last)
        x, h = outs
    xp = x.reshape(batch, seq, d)
    xs = h.reshape(dec_batch, dec_seq, d)
    return (xp, xs, jnp.stack(ks_out, axis=1), jnp.stack(vs_out, axis=1), jnp.stack(ss_out, axis=1))
```

```python
import functools
import math

import jax
import jax.numpy as jnp
from jax import lax
from jax.experimental import pallas as pl
from jax.experimental.pallas import tpu as pltpu

F32 = jnp.float32
BF16 = jnp.bfloat16

GRID_W = 64
ROPE_BASE = 10000.0
RET_CHUNK = 128
N_GROUPS = 4
TOP_K = 2
LN_EPS = 1e-5
HEAD_W = 128
MOE_ROWS = 256
VMEM_LIMIT = 56 * 1024 * 1024

_NT = (((1,), (1,)), ((), ()))
Q_SCALE = 0.125 * math.log2(math.e)


def _tile(n, pref, mult=8):
    t = min(n, pref)
    while n % t or (t % mult and t != n):
        t -= 1
    return t


def _params(sem, vmem=VMEM_LIMIT, flags=None):
    return pltpu.CompilerParams(dimension_semantics=sem, vmem_limit_bytes=vmem, flags=flags)


def _mod_kernel(c_ref, w_ref, b_ref, o_ref):
    c = c_ref[...]
    s = c * jax.nn.sigmoid(c)
    o_ref[...] = jnp.dot(s, w_ref[...], precision=lax.Precision.HIGHEST,
                         preferred_element_type=F32) + b_ref[...]


def _modulation(cs, w_mod, b_mod):
    depth, d, n = w_mod.shape
    rows = cs.shape[0]
    tn = _tile(n, 1024, 128)
    return pl.pallas_call(
        _mod_kernel,
        grid=(depth, n // tn),
        in_specs=[pl.BlockSpec((rows, d), lambda l, j: (0, 0)),
                  pl.BlockSpec((None, d, tn), lambda l, j: (l, 0, j)),
                  pl.BlockSpec((None, 1, tn), lambda l, j: (l, 0, j))],
        out_specs=pl.BlockSpec((None, rows, tn), lambda l, j: (l, 0, j)),
        out_shape=jax.ShapeDtypeStruct((depth, rows, n), F32),
        compiler_params=_params(("parallel", "parallel")),
        name="adaln_mod",
    )(cs, w_mod, b_mod.reshape(depth, 1, n))


class _Tokens:
    def __init__(self, batch, seq, dec_batch, dec_seq, n_cond):
        self.t_p = batch * seq
        self.t_s = dec_batch * dec_seq
        self.t = self.t_p + self.t_s
        self.dec_seq = dec_seq
        self.n_cond = n_cond

    def tile(self, pref):
        return _tile(math.gcd(self.t_p, self.dec_seq), pref)

    def group_spec(self, tm, w, prompt):
        n_p = self.t_p // tm
        n_s = self.t_s // tm
        if prompt:
            return pl.BlockSpec((tm, w), lambda i: (jnp.minimum(i, n_p - 1), 0))
        return pl.BlockSpec((tm, w), lambda i: (jnp.clip(i - n_p, 0, n_s - 1), 0))

    def mod_spec(self, layer, which, tm, d):
        t_p, dec_seq, n_cond = self.t_p, self.dec_seq, self.n_cond

        def idx(i, *_):
            r = i * tm
            cond = jnp.where(r < t_p, 0, 1 + (r - t_p) // dec_seq)
            return ((layer * n_cond + cond) * 6 + which, 0, 0)

        return pl.BlockSpec((None, 1, d), idx)


def _vec_spec(d):
    return pl.BlockSpec((1, d), lambda i, *_: (0, 0))


def _premod_kernel(xp_ref, xs_ref, sh_ref, sc_ref, x_ref, h_ref, *, n_p):
    x = jnp.where(pl.program_id(0) < n_p, xp_ref[...], xs_ref[...])
    x_ref[...] = x
    h_ref[...] = (x * (1.0 + sc_ref[...]) + sh_ref[...]).astype(BF16)


def _premod(x_p, x_s, mod6, tok):
    d = x_p.shape[1]
    tm = tok.tile(512)
    n_p = tok.t_p // tm
    row = pl.BlockSpec((tm, d), lambda i: (i, 0))
    return pl.pallas_call(
        functools.partial(_premod_kernel, n_p=n_p),
        grid=(tok.t // tm,),
        in_specs=[tok.group_spec(tm, d, True), tok.group_spec(tm, d, False),
                  tok.mod_spec(0, 0, tm, d), tok.mod_spec(0, 1, tm, d)],
        out_specs=[row, row],
        out_shape=[jax.ShapeDtypeStruct((tok.t, d), F32), jax.ShapeDtypeStruct((tok.t, d), BF16)],
        compiler_params=_params(("parallel",)),
        name="premod",
    )(x_p, x_s, mod6, mod6)


def _matmul_kernel(a_ref, w_ref, o_ref):
    o_ref[...] = jnp.dot(a_ref[...], w_ref[...], preferred_element_type=F32)


def _in_proj(h, w_in_b, layer):
    t, d = h.shape
    n = w_in_b.shape[-1]
    tm = _tile(t, 1024)
    tn = _tile(n, 1024, 128)
    return pl.pallas_call(
        _matmul_kernel,
        grid=(t // tm, n // tn),
        in_specs=[pl.BlockSpec((tm, d), lambda i, j: (i, 0)),
                  pl.BlockSpec((None, d, tn), lambda i, j: (layer, 0, j))],
        out_specs=pl.BlockSpec((tm, tn), lambda i, j: (i, j)),
        out_shape=jax.ShapeDtypeStruct((t, n), F32),
        compiler_params=_params(("parallel", "parallel")),
        name="in_proj",
    )(h, w_in_b)


def _rope(x, cos, sa, sb):
    return x * cos + pltpu.roll(x, HEAD_W - 16, 1) * sa + pltpu.roll(x, 16, 1) * sb


def _rope_q_kernel(z_ref, cos_ref, sa_ref, sb_ref, o_ref, *, n_heads):
    cos, sa, sb = cos_ref[...], sa_ref[...], sb_ref[...]
    for c in range(n_heads):
        cols = slice(c * HEAD_W, (c + 1) * HEAD_W)
        o_ref[:, cols] = (_rope(z_ref[:, cols], cos, sa, sb) * Q_SCALE).astype(BF16)


def _rope_q(z, tables, tok, h_diff):
    w = h_diff * HEAD_W
    tm = _tile(tok.dec_seq, 512)
    off = tok.t_p // tm
    nseq = tok.dec_seq // tm
    tab = pl.BlockSpec((tm, HEAD_W), lambda i: (i % nseq, 0))
    return pl.pallas_call(
        functools.partial(_rope_q_kernel, n_heads=h_diff),
        grid=(tok.t_s // tm,),
        in_specs=[pl.BlockSpec((tm, w), lambda i: (off + i, 0)), tab, tab, tab],
        out_specs=pl.BlockSpec((tm, w), lambda i: (i, 0)),
        out_shape=jax.ShapeDtypeStruct((tok.t_s, w), BF16),
        compiler_params=_params(("parallel",)),
        name="rope_q",
    )(z, *tables)


def _lam_value(lamp_ref, lam_init):
    lp = lamp_ref[...]
    a = jnp.sum(lp[0:1] * lp[1:2], axis=-1, keepdims=True)
    b = jnp.sum(lp[2:3] * lp[3:4], axis=-1, keepdims=True)
    return jnp.exp(a) - jnp.exp(b) + lam_init


def _softmax_pv(qc, chunks):
    m = acc = None
    for kk, vv in chunks:
        kk, vv = kk(), vv()
        s = lax.dot_general(qc, kk, _NT, preferred_element_type=F32)
        cm = jnp.max(s, axis=-1, keepdims=True)
        m_new = cm if m is None else jnp.maximum(m, cm)
        pv = jnp.dot(jnp.exp2(s - m_new).astype(BF16), vv, preferred_element_type=F32)
        acc = pv if m is None else jnp.exp2(m - m_new) * acc + pv
        m = m_new
    return acc[:, :HEAD_W] / acc[:, HEAD_W:]


def _q_halves(q):
    lane = lax.broadcasted_iota(jnp.int32, q.shape, 1)
    zero = jnp.zeros_like(q)
    return jnp.where(lane < HEAD_W // 2, q, zero), jnp.where(lane >= HEAD_W // 2, q, zero)


def _diff_norm(o1, o2, lam, g, lam_init):
    o = o1 - lam * o2
    o = o * lax.rsqrt(jnp.mean(o * o, axis=-1, keepdims=True) + LN_EPS) * g
    return o * (1.0 - lam_init)


def _diff_out(q, chunks, lam, g, lam_init):
    q1, q2 = _q_halves(q)
    return _diff_norm(_softmax_pv(q1, chunks), _softmax_pv(q2, chunks), lam, g, lam_init)


def _half_sq_norms(x):
    xf = x.astype(F32)
    sq = xf * xf
    lane = lax.broadcasted_iota(jnp.int32, sq.shape, 1)
    lo = jnp.sum(jnp.where(lane < HEAD_W // 2, sq, 0.0), axis=-1, keepdims=True)
    return lo, jnp.sum(sq, axis=-1, keepdims=True) - lo


def _softmax_pv_bounded(qc, bound, key_chunks, p_ref, vall_ref):
    col = 0
    for kk in key_chunks:
        kk = kk()
        s = lax.dot_general(qc, kk, _NT, preferred_element_type=F32)
        p_ref[:, col:col + kk.shape[0]] = jnp.exp2(s - bound).astype(BF16)
        col += kk.shape[0]
    acc = jnp.dot(p_ref[...], vall_ref[...], preferred_element_type=F32)
    return acc[:, :HEAD_W], acc[:, HEAD_W:]


def _attn_prompt_kernel(lamp_ref, q_ref, k_ref, v_ref, g_ref, o_ref, ok_ref, ov_ref, *, lam_init, hp):
    lam = _lam_value(lamp_ref, lam_init)
    g = g_ref[...]
    for j in range(hp):
        cols = slice(j * HEAD_W, (j + 1) * HEAD_W)
        k = k_ref[:, cols]
        v = v_ref[:, cols]
        ok_ref[j] = k
        ov_ref[j] = v
        q = (q_ref[:, cols] * Q_SCALE).astype(BF16)
        v1 = jnp.concatenate([v.astype(BF16), jnp.ones(v.shape, BF16)], axis=1)
        kb = k.astype(BF16)
        o = _diff_out(q, [(lambda kb=kb: kb, lambda v1=v1: v1)], lam, g, lam_init)
        o_ref[:, cols] = o.astype(BF16)


def _attn_prompt(z, lamp, subln_g, layer, lam_init, batch, seq, h_diff):
    t_p = batch * seq
    hp = _tile(h_diff, 4, 1)
    w = hp * HEAD_W
    ng = h_diff // hp
    blk = lambda c0: pl.BlockSpec((seq, w), lambda b, g: (b, c0 * ng + g))
    own = pl.BlockSpec((None, hp, seq, HEAD_W), lambda b, g: (b, g, 0, 0))
    return pl.pallas_call(
        functools.partial(_attn_prompt_kernel, lam_init=lam_init, hp=hp),
        grid=(batch, ng),
        in_specs=[pl.BlockSpec((None, 4, HEAD_W // 2), lambda b, g: (layer, 0, 0)),
                  blk(0), blk(1), blk(2),
                  pl.BlockSpec((None, 1, HEAD_W), lambda b, g: (layer, 0, 0))],
        out_specs=[pl.BlockSpec((seq, w), lambda b, g: (b, g)), own, own],
        out_shape=[jax.ShapeDtypeStruct((t_p, h_diff * HEAD_W), BF16),
                   jax.ShapeDtypeStruct((batch, h_diff, seq, HEAD_W), F32),
                   jax.ShapeDtypeStruct((batch, h_diff, seq, HEAD_W), F32)],
        compiler_params=_params(("parallel", "parallel")),
        name="diff_attn_prompt",
    )(lamp, z, z, z, subln_g)


_MIN_ROW_SUM = 2.0 ** -64


def _attn_sample_kernel(lamp_ref, q_ref, k_ref, v_ref, ck_ref, cv_ref, g_ref, cos_ref, sa_ref, sb_ref,
                        o_ref, kmax_ref, kr_ref, vall_ref, p1_ref, p2_ref, *, lam_init, kc):
    n = k_ref.shape[0]
    past = ck_ref.shape[0]

    @pl.when(pl.program_id(2) == 0)
    def _():
        kr_ref[...] = _rope(k_ref[...], cos_ref[...], sa_ref[...], sb_ref[...]).astype(BF16)
        vall_ref[:past, :] = cv_ref[...]
        vall_ref[past:, :HEAD_W] = v_ref[...].astype(BF16)
        vall_ref[past:, HEAD_W:] = jnp.ones((n, HEAD_W), BF16)
        lo_c, hi_c = _half_sq_norms(ck_ref[...])
        lo_k, hi_k = _half_sq_norms(kr_ref[...])
        lo = jnp.maximum(jnp.max(lo_c, axis=0, keepdims=True), jnp.max(lo_k, axis=0, keepdims=True))
        hi = jnp.maximum(jnp.max(hi_c, axis=0, keepdims=True), jnp.max(hi_k, axis=0, keepdims=True))
        kmax_ref[0] = jnp.broadcast_to(jnp.sqrt(lo), kmax_ref.shape[1:])
        kmax_ref[1] = jnp.broadcast_to(jnp.sqrt(hi), kmax_ref.shape[1:])

    keys = [lambda: ck_ref[...]]
    vals = [lambda: vall_ref[:past, :]]
    for c in range(n // kc):
        keys.append(lambda c=c: kr_ref[c * kc:(c + 1) * kc, :])
        vals.append(lambda c=c: vall_ref[past + c * kc:past + (c + 1) * kc, :])
    lam = _lam_value(lamp_ref, lam_init)
    g = g_ref[...]
    q = q_ref[...]

    q1, q2 = _q_halves(q)
    qlo, qhi = _half_sq_norms(q)
    a1, l1 = _softmax_pv_bounded(q1, jnp.sqrt(qlo) * kmax_ref[0, 0:1, 0:1], keys, p1_ref, vall_ref)
    a2, l2 = _softmax_pv_bounded(q2, jnp.sqrt(qhi) * kmax_ref[1, 0:1, 0:1], keys, p2_ref, vall_ref)
    o_ref[...] = _diff_norm(a1 / l1, a2 / l2, lam, g, lam_init).astype(BF16)

    @pl.when(jnp.logical_not(jnp.min(jnp.minimum(l1, l2)) >= _MIN_ROW_SUM))
    def _():
        o_ref[...] = _diff_out(q, list(zip(keys, vals)), lam, g, lam_init).astype(BF16)


def _attn_sample(qb, z, row0, tables, ck, cv, lamp, subln_g, layer, lam_init, dec_batch, dec_seq, h_diff):
    assert row0 % dec_seq == 0
    tq = _tile(dec_seq, 1024)
    kc = _tile(dec_seq, 512)
    nq = dec_seq // tq
    s0 = row0 // dec_seq
    past = ck.shape[3]
    ctx = lambda w: pl.BlockSpec((None, None, None, past, w), lambda b, h, i: (b, layer, h, 0, 0))
    own = lambda g: pl.BlockSpec((dec_seq, HEAD_W), lambda b, h, i: (s0 + b, g * h_diff + h))
    tab_k = pl.BlockSpec((dec_seq, HEAD_W), lambda b, h, i: (0, 0))
    return pl.pallas_call(
        functools.partial(_attn_sample_kernel, lam_init=lam_init, kc=kc),
        grid=(dec_batch, h_diff, nq),
        in_specs=[pl.BlockSpec((None, 4, HEAD_W // 2), lambda b, h, i: (layer, 0, 0)),
                  pl.BlockSpec((tq, HEAD_W), lambda b, h, i: (b * nq + i, h)),
                  own(1), own(2), ctx(HEAD_W), ctx(2 * HEAD_W),
                  pl.BlockSpec((None, 1, HEAD_W), lambda b, h, i: (layer, 0, 0)),
                  tab_k, tab_k, tab_k],
        out_specs=pl.BlockSpec((tq, HEAD_W), lambda b, h, i: (b * nq + i, h)),
        out_shape=jax.ShapeDtypeStruct((dec_batch * dec_seq, h_diff * HEAD_W), BF16),
        scratch_shapes=[pltpu.VMEM((2, 8, HEAD_W), F32), pltpu.VMEM((dec_seq, HEAD_W), BF16),
                        pltpu.VMEM((past + dec_seq, 2 * HEAD_W), BF16),
                        pltpu.VMEM((tq, past + dec_seq), BF16), pltpu.VMEM((tq, past + dec_seq), BF16)],
        compiler_params=_params(("parallel", "parallel", "arbitrary")),
        name="diff_attn_sample",
    )(lamp, qb, z, z, ck, cv, subln_g, *tables)


def _ret_kernel(*refs, has_state, emit_state, nc, scale, hp):
    q_ref, k_ref, v_ref, g_ref, if_ref, ib_ref, cols_ref = refs[:7]
    refs = refs[7:]
    if has_state:
        s0_ref, refs = refs[0], refs[1:]
    o_ref, refs = refs[0], refs[1:]
    if emit_state:
        s_ref, refs = refs[0], refs[1:]
    of_ref, ob_ref, sf_ref, sb_ref = refs
    c_ = RET_CHUNK
    unroll = _tile(nc, 4, 1)

    def chunk_rows(c):
        return pl.ds(pl.multiple_of(c * c_, c_), c_)

    for j in range(hp):
        lanes = slice(j * HEAD_W, (j + 1) * HEAD_W)
        intra_f = if_ref[j]
        intra_b = ib_ref[j]
        cols = cols_ref[j]
        qdf, kdf, qdb, kdb = cols[:, 0:1], cols[:, 1:2], cols[:, 2:3], cols[:, 3:4]
        cdf, cdb = cols[0:1, 4:5], cols[0:1, 5:6]

        def intra_and_state(c, s, intra, kd, cd, out_ref, st_ref):
            rows = chunk_rows(c)
            st_ref[c] = s.astype(BF16)
            q = q_ref[rows, lanes].astype(BF16)
            k = k_ref[rows, lanes] * scale
            v = v_ref[rows, lanes].astype(BF16)
            a = lax.dot_general(q, k.astype(BF16), _NT, preferred_element_type=F32) * intra
            out_ref[rows, :] = jnp.dot(a.astype(BF16), v, preferred_element_type=F32)
            kt = jnp.transpose(k * kd).astype(BF16)
            return cd * s + jnp.dot(kt, v, preferred_element_type=F32)

        def scan(i, carry):
            sf, sb = carry
            sf = intra_and_state(i, sf, intra_f, kdf, cdf, of_ref, sf_ref)
            sb = intra_and_state(nc - 1 - i, sb, intra_b, kdb, cdb, ob_ref, sb_ref)
            return sf, sb

        if has_state:
            init = (s0_ref[0, j], s0_ref[1, j])
        else:
            init = (jnp.zeros((HEAD_W, HEAD_W), F32), jnp.zeros((HEAD_W, HEAD_W), F32))
        sf, sb = lax.fori_loop(0, nc, scan, init, unroll=unroll)
        if emit_state:
            s_ref[0, j] = sf
            s_ref[1, j] = sb

        def finish(c, carry):
            rows = chunk_rows(c)
            q = q_ref[rows, lanes]
            o = of_ref[rows, :] + ob_ref[rows, :]
            o = o + jnp.dot((q * qdf).astype(BF16), sf_ref[c], preferred_element_type=F32)
            o = o + jnp.dot((q * qdb).astype(BF16), sb_ref[c], preferred_element_type=F32)
            mu = jnp.mean(o, axis=-1, keepdims=True)
            oc = o - mu
            o = oc * lax.rsqrt(jnp.mean(oc * oc, axis=-1, keepdims=True) + LN_EPS)
            g = g_ref[rows, lanes]
            o_ref[rows, lanes] = (o * (g * jax.nn.sigmoid(g))).astype(BF16)
            return carry

        lax.fori_loop(0, nc, finish, 0, unroll=unroll)


def _retention(z, tabs, state, layer, n_seq, seq, row0, h_diff, h_ret, emit_state):
    intra_f, intra_b, cols = tabs
    c_ = RET_CHUNK
    nc = seq // c_
    assert row0 % seq == 0
    off = row0 // seq
    hp = _tile(h_ret, 4, 1) if nc <= 4 else 1
    w = hp * HEAD_W
    ng = h_ret // hp
    assert (3 * h_diff) % hp == 0
    c0 = 3 * h_diff // hp
    blk = lambda q: pl.BlockSpec((seq, w), lambda b, g: (off + b, c0 + q * ng + g))
    htab = lambda shape: pl.BlockSpec((hp,) + shape, lambda b, g: (g,) + (0,) * len(shape))
    in_specs = [blk(0), blk(1), blk(2), blk(3), htab((c_, c_)), htab((c_, c_)), htab((c_, 8))]
    args = [z, z, z, z, intra_f, intra_b, cols]
    if state is not None:
        in_specs.append(pl.BlockSpec((None, None, 2, hp, HEAD_W, HEAD_W), lambda b, g: (b, layer, 0, g, 0, 0)))
        args.append(state)
    out_specs = [pl.BlockSpec((seq, w), lambda b, g: (b, g))]
    out_shape = [jax.ShapeDtypeStruct((n_seq * seq, h_ret * HEAD_W), BF16)]
    if emit_state:
        out_specs.append(pl.BlockSpec((None, 2, hp, HEAD_W, HEAD_W), lambda b, g: (b, 0, g, 0, 0)))
        out_shape.append(jax.ShapeDtypeStruct((n_seq, 2, h_ret, HEAD_W, HEAD_W), F32))

    return pl.pallas_call(
        functools.partial(_ret_kernel, has_state=state is not None, emit_state=emit_state, nc=nc,
                          scale=HEAD_W ** -0.5, hp=hp),
        grid=(n_seq, ng),
        in_specs=in_specs,
        out_specs=out_specs,
        out_shape=out_shape,
        scratch_shapes=[pltpu.VMEM((seq, HEAD_W), F32), pltpu.VMEM((seq, HEAD_W), F32),
                        pltpu.VMEM((nc, HEAD_W, HEAD_W), BF16), pltpu.VMEM((nc, HEAD_W, HEAD_W), BF16)],
        compiler_params=_params(("parallel", "parallel")),
        name="retention_%d" % seq,
    )(*args)


def _layer_norm(v, g, b):
    mu = jnp.mean(v, axis=-1, keepdims=True)
    vc = v - mu
    var = jnp.mean(vc * vc, axis=-1, keepdims=True)
    return vc * lax.rsqrt(var + LN_EPS) * g + b


def _route(p, n_exp):
    per = n_exp // N_GROUPS
    rows = [p[e:e + 1, :] for e in range(n_exp)]
    scores = []
    for g in range(N_GROUPS):
        a, b, c, d = rows[per * g:per * g + per]
        hi1, lo1, hi2, lo2 = jnp.maximum(a, b), jnp.minimum(a, b), jnp.maximum(c, d), jnp.minimum(c, d)
        scores.append(jnp.maximum(hi1, hi2) + jnp.maximum(jnp.minimum(hi1, hi2), jnp.maximum(lo1, lo2)))
    best = scores[0]
    gsel = jnp.zeros(best.shape, jnp.int32)
    for g in range(1, N_GROUPS):
        better = scores[g] > best
        best = jnp.where(better, scores[g], best)
        gsel = jnp.where(better, g, gsel)
    vals = []
    for j in range(per):
        v = rows[(N_GROUPS - 1) * per + j]
        for g in range(N_GROUPS - 2, -1, -1):
            v = jnp.where(gsel == g, rows[g * per + j], v)
        vals.append(v)
    b1, i1 = vals[0], jnp.zeros(best.shape, jnp.int32)
    for j in range(1, per):
        gt = vals[j] > b1
        b1 = jnp.where(gt, vals[j], b1)
        i1 = jnp.where(gt, j, i1)
    b2, i2 = jnp.full(best.shape, -1.0, F32), jnp.zeros(best.shape, jnp.int32)
    for j in range(per):
        gt = jnp.where(i1 == j, -2.0, vals[j]) > b2
        b2 = jnp.where(gt, vals[j], b2)
        i2 = jnp.where(gt, j, i2)
    tot = b1 + b2
    return gsel * per + i1, gsel * per + i2, b1 / tot, b2 / tot


def _outproj_kernel(odp_ref, ods_ref, orp_ref, ors_ref, wd_ref, wr_ref, x_ref, ga_ref, shf_ref, scf_ref,
                    lng_ref, lnb_ref, wrt_ref, brt_ref, x1_ref, h2_ref, idx_ref, gate_ref, *, alpha, n_p):
    prompt = pl.program_id(0) < n_p
    o_d = jnp.where(prompt, odp_ref[...], ods_ref[...])
    o_r = jnp.where(prompt, orp_ref[...], ors_ref[...])
    mix = jnp.dot(o_d, wd_ref[...], preferred_element_type=F32)
    mix = mix + jnp.dot(o_r, wr_ref[...], preferred_element_type=F32)
    x1 = _layer_norm(alpha * x_ref[...] + ga_ref[...] * mix, lng_ref[...], lnb_ref[...])
    x1_ref[...] = x1
    h2 = x1 * (1.0 + scf_ref[...]) + shf_ref[...]
    h2_ref[...] = h2
    h_hi = h2.astype(BF16)
    h_lo = (h2 - h_hi.astype(F32)).astype(BF16)
    w_hi, w_lo = wrt_ref[0], wrt_ref[1]
    logits = (lax.dot_general(w_hi, h_hi, _NT, preferred_element_type=F32)
              + lax.dot_general(w_lo, h_hi, _NT, preferred_element_type=F32)
              + lax.dot_general(w_hi, h_lo, _NT, preferred_element_type=F32)) + brt_ref[...]
    e = jnp.exp(logits - jnp.max(logits, axis=0, keepdims=True))
    p = e / jnp.sum(e, axis=0, keepdims=True)
    e1, e2, g1, g2 = _route(p, logits.shape[0])
    idx_ref[0:1, :] = e1
    idx_ref[1:2, :] = e2
    gate_ref[0:1, :] = g1
    gate_ref[1:2, :] = g2


def _out_proj(od_p, od_s, or_p, or_s, w_out_b, x, mod6, ln_g, ln_b, w_router_t, b_router_t, layer, tok, alpha):
    t, d = x.shape
    wd = od_p.shape[1]
    wr = or_p.shape[1]
    n_exp = w_router_t.shape[1]
    tm = tok.tile(512)
    row = lambda w: pl.BlockSpec((tm, w), lambda i: (i, 0))
    lnv = pl.BlockSpec((None, None, 1, d), lambda i: (layer, 0, 0, 0))
    return pl.pallas_call(
        functools.partial(_outproj_kernel, alpha=alpha, n_p=tok.t_p // tm),
        grid=(t // tm,),
        in_specs=[tok.group_spec(tm, wd, True), tok.group_spec(tm, wd, False),
                  tok.group_spec(tm, wr, True), tok.group_spec(tm, wr, False),
                  pl.BlockSpec((None, wd, d), lambda i: (layer, 0, 0)),
                  pl.BlockSpec((None, wr, d), lambda i: (layer, wd // wr, 0)),
                  row(d),
                  tok.mod_spec(layer, 2, tm, d), tok.mod_spec(layer, 3, tm, d), tok.mod_spec(layer, 4, tm, d),
                  lnv, lnv,
                  pl.BlockSpec((2, n_exp, d), lambda i: (0, 0, 0)),
                  pl.BlockSpec((n_exp, 1), lambda i: (0, 0))],
        out_specs=[row(d), row(d),
                   pl.BlockSpec((TOP_K, tm), lambda i: (0, i)),
                   pl.BlockSpec((TOP_K, tm), lambda i: (0, i))],
        out_shape=[jax.ShapeDtypeStruct((t, d), F32), jax.ShapeDtypeStruct((t, d), F32),
                   jax.ShapeDtypeStruct((TOP_K, t), jnp.int32), jax.ShapeDtypeStruct((TOP_K, t), F32)],
        compiler_params=_params(("parallel",)),
        name="out_proj_ln_router",
    )(od_p, od_s, or_p, or_s, w_out_b, w_out_b, x, mod6, mod6, mod6, ln_g, ln_b, w_router_t, b_router_t)


def _moe_kernel(be_ref, cnt_ref, tok0_ref, tokn_ref, dst_ref, gate_ref, x_hbm, w1_ref, w3_ref, w2_ref, y_hbm,
                xbuf, ybuf, gsem, ssem, *, n_out):
    del be_ref
    s = pl.program_id(0)
    rows = xbuf.shape[1]
    c_old, c_cur, c_next = cnt_ref[s], cnt_ref[s + 2], cnt_ref[s + 3]

    def gather_copy(tok_ref, r, p):
        return pltpu.make_async_copy(x_hbm.at[pl.ds(tok_ref[0, 0, r], 1), :], xbuf.at[p, pl.ds(r, 1), :], gsem.at[p])

    def scatter_copy(r, p):
        return pltpu.make_async_copy(ybuf.at[p, pl.ds(r, 1), :], y_hbm.at[pl.ds(dst_ref[0, 0, r], 1), :], ssem.at[p])

    @pl.when(s == 0)
    def _():
        for p in range(2):
            ybuf[p] = jnp.zeros(ybuf.shape[1:], F32)
            spare = pltpu.make_async_copy(ybuf.at[p], y_hbm.at[pl.ds(n_out + p * rows, rows), :], ssem.at[p])
            spare.start()
            spare.wait()

        def first(r, c):
            gather_copy(tok0_ref, r, 0).start()
            return c

        lax.fori_loop(0, rows, first, 0)

    for p in range(2):
        mine = s % 2 == p

        @pl.when(jnp.logical_and(mine, c_old > 0))
        def _():
            pltpu.make_async_copy(ybuf.at[p], y_hbm.at[pl.ds(0, rows), :], ssem.at[p]).wait()

        @pl.when(jnp.logical_and(mine, c_next > 0))
        def _():
            for r in range(rows):
                gather_copy(tokn_ref, r, 1 - p).start()

        @pl.when(jnp.logical_and(mine, c_cur > 0))
        def _():
            pltpu.make_async_copy(x_hbm.at[pl.ds(0, rows), :], xbuf.at[p], gsem.at[p]).wait()
            x = xbuf[p].astype(BF16)
            a = jnp.dot(x, w1_ref[...], preferred_element_type=F32)
            g = jnp.dot(x, w3_ref[...], preferred_element_type=F32)
            h = (a * jax.nn.sigmoid(a) * g).astype(BF16)
            ybuf[p] = jnp.dot(h, w2_ref[...], preferred_element_type=F32) * gate_ref[...]
            for r in range(rows):
                scatter_copy(r, p).start()


def _experts(h2, block_e, block_cnt, slot_tok, slot_dst, slot_gate, w1b, w3b, w2b, layer, n_out):
    t, d = h2.shape
    nb = block_e.shape[0]
    ff = w1b.shape[-1]
    rows = MOE_ROWS
    last = nb - 1
    cnt_pad = jnp.concatenate([jnp.zeros((2,), jnp.int32), block_cnt, jnp.zeros((3,), jnp.int32)])
    smem = lambda off: pl.BlockSpec((1, 1, rows), lambda s, be, n: (jnp.minimum(s + off, last), 0, 0),
                                    memory_space=pltpu.SMEM)
    wspec = lambda shape: pl.BlockSpec((None, None) + shape,
                                       lambda s, be, n: (layer, be[jnp.minimum(s, last)], 0, 0))
    grid_spec = pltpu.PrefetchScalarGridSpec(
        num_scalar_prefetch=2,
        grid=(nb + 2,),
        in_specs=[smem(0), smem(1), smem(0),
                  pl.BlockSpec((rows, 1), lambda s, be, n: (jnp.minimum(s, last), 0)),
                  pl.BlockSpec(memory_space=pl.ANY),
                  wspec((d, ff)), wspec((d, ff)), wspec((ff, d))],
        out_specs=pl.BlockSpec(memory_space=pl.ANY),
        scratch_shapes=[pltpu.VMEM((2, rows, d), F32), pltpu.VMEM((2, rows, d), F32),
                        pltpu.SemaphoreType.DMA((2,)), pltpu.SemaphoreType.DMA((2,))])
    tok3 = slot_tok.reshape(nb, 1, rows)
    return pl.pallas_call(
        functools.partial(_moe_kernel, n_out=n_out),
        grid_spec=grid_spec,
        out_shape=jax.ShapeDtypeStruct((n_out + 2 * rows, d), F32),
        compiler_params=_params(("arbitrary",)),
        name="moe_experts",
    )(block_e, cnt_pad, tok3, tok3, slot_dst.reshape(nb, 1, rows), slot_gate.reshape(nb * rows, 1),
      h2, w1b, w3b, w2b)


def _dispatch_plan(idx, gate, n_exp, t):
    rows = MOE_ROWS
    a = t * TOP_K
    flat_e = idx.T.reshape(a)
    flat_g = gate.T.reshape(a)
    _, order = lax.sort((flat_e, jnp.arange(a, dtype=jnp.int32)), num_keys=1, is_stable=True)
    experts = jnp.arange(n_exp, dtype=jnp.int32)
    counts = jnp.sum((flat_e[None, :] == experts[:, None]).astype(jnp.int32), axis=1)
    padded = (counts + rows - 1) // rows * rows
    ends_p = jnp.cumsum(padded)
    starts_p = ends_p - padded
    starts = jnp.cumsum(counts) - counts
    nb = -(-a // rows) + n_exp
    blk = jnp.arange(nb, dtype=jnp.int32)
    block_e = jnp.minimum(jnp.sum((blk[:, None] * rows >= ends_p[None, :]).astype(jnp.int32), axis=1), n_exp - 1)
    onehot = block_e[:, None] == experts[None, :]
    pick = lambda tab: jnp.sum(jnp.where(onehot, tab[None, :], 0), axis=1)
    within = (blk * rows - pick(starts_p))[:, None] + jnp.arange(rows, dtype=jnp.int32)[None, :]
    valid = within < pick(counts)[:, None]
    src = order[jnp.clip(pick(starts)[:, None] + within, 0, a - 1)]
    slot_tok = jnp.where(valid, src // TOP_K, 0)
    spare = a + (blk % 2)[:, None] * rows + jnp.arange(rows, dtype=jnp.int32)[None, :]
    slot_dst = jnp.where(valid, (src % TOP_K) * t + src // TOP_K, spare)
    slot_gate = jnp.where(valid, flat_g[src], 0.0)
    block_cnt = jnp.sum(valid.astype(jnp.int32), axis=1)
    return block_e, block_cnt, slot_tok, slot_dst, slot_gate


def _ln2_kernel(*refs, alpha, emit_h, n_p):
    x1_ref, y0_ref, y1_ref, gf_ref, lng_ref, lnb_ref = refs[:6]
    y = y0_ref[...] + y1_ref[...]
    x2 = _layer_norm(alpha * x1_ref[...] + gf_ref[...] * y, lng_ref[...], lnb_ref[...])
    if emit_h:
        sh_ref, sc_ref, x2_ref, h_ref = refs[6:]
        h_ref[...] = (x2 * (1.0 + sc_ref[...]) + sh_ref[...]).astype(BF16)
        x2_ref[...] = x2
    else:
        xp_ref, xs_ref = refs[6:]

        @pl.when(pl.program_id(0) < n_p)
        def _():
            xp_ref[...] = x2

        @pl.when(pl.program_id(0) >= n_p)
        def _():
            xs_ref[...] = x2


def _ln2(x1, y2, mod6, ln_g, ln_b, layer, tok, alpha, last):
    t, d = x1.shape
    tm = tok.tile(512)
    nt = t // tm
    row = lambda off: pl.BlockSpec((tm, d), lambda i: (off + i, 0))
    lnv = pl.BlockSpec((None, None, 1, d), lambda i: (layer, 1, 0, 0))
    in_specs = [row(0), row(0), row(nt), tok.mod_spec(layer, 5, tm, d), lnv, lnv]
    args = [x1, y2, y2, mod6, ln_g, ln_b]
    if last:
        out_specs = [tok.group_spec(tm, d, True), tok.group_spec(tm, d, False)]
        out_shape = [jax.ShapeDtypeStruct((tok.t_p, d), F32), jax.ShapeDtypeStruct((tok.t_s, d), F32)]
    else:
        in_specs += [tok.mod_spec(layer + 1, 0, tm, d), tok.mod_spec(layer + 1, 1, tm, d)]
        args += [mod6, mod6]
        out_specs = [row(0), row(0)]
        out_shape = [jax.ShapeDtypeStruct((t, d), F32), jax.ShapeDtypeStruct((t, d), BF16)]
    return pl.pallas_call(
        functools.partial(_ln2_kernel, alpha=alpha, emit_h=not last, n_p=tok.t_p // tm),
        grid=(nt,),
        in_specs=in_specs,
        out_specs=out_specs,
        out_shape=out_shape,
        compiler_params=_params(("arbitrary",)),
        name="ln2_mod",
    )(*args)


def _rope_tables(n_tok):
    rows = n_tok // GRID_W
    axis = HEAD_W // 4
    r = jnp.repeat(jnp.arange(rows, dtype=F32), GRID_W)
    col = jnp.tile(jnp.arange(GRID_W, dtype=F32), rows)
    inv = ROPE_BASE ** (-jnp.arange(0, axis, 2, dtype=F32) / axis)
    ar = r[:, None] * inv[None]
    ac = col[:, None] * inv[None]
    ang = jnp.concatenate([ar, ar, ac, ac], -1)
    cos, sin = jnp.cos(ang), jnp.sin(ang)
    cos, sin = jnp.tile(cos, (1, 2)), jnp.tile(sin, (1, 2))
    low = (jnp.arange(HEAD_W) % (axis)) < axis // 2
    return cos, jnp.where(low, -sin, 0.0), jnp.where(low, 0.0, sin)


def _ret_tables(h_ret):
    c_ = RET_CHUNK
    h = jnp.arange(h_ret, dtype=F32)
    lg_f = jnp.log(1.0 - jnp.exp2(-5.0 - h))
    lg_b = jnp.log(1.0 - jnp.exp2(-5.5 - h))
    pos = jnp.arange(c_, dtype=F32)
    dist = pos[:, None] - pos[None, :]
    intra_f = jnp.where(dist >= 0, jnp.exp(lg_f[:, None, None] * jnp.maximum(dist, 0.0)), 0.0)
    intra_b = jnp.where(dist <= 0, jnp.exp(lg_b[:, None, None] * jnp.maximum(-dist, 0.0)), 0.0)
    qd_f = jnp.exp(lg_f[:, None] * (pos + 1.0))
    kd_f = jnp.exp(lg_f[:, None] * (c_ - 1.0 - pos))
    qd_b = jnp.exp(lg_b[:, None] * (c_ - pos))
    kd_b = jnp.exp(lg_b[:, None] * pos)
    cd_f = jnp.broadcast_to(jnp.exp(lg_f * c_)[:, None], (h_ret, c_))
    cd_b = jnp.broadcast_to(jnp.exp(lg_b * c_)[:, None], (h_ret, c_))
    zero = jnp.zeros((h_ret, c_), F32)
    cols = jnp.stack([qd_f, kd_f, qd_b, kd_b, cd_f, cd_b, zero, zero], axis=2)
    return intra_f, intra_b, cols


def kernel(x_prompt, x_sample, c, cache_k, cache_v, state_ret, c_ctx, w_mod, b_mod, w_in, w_out,
           lam_q1, lam_k1, lam_q2, lam_k2, subln_g, ln_g, ln_b, w_router, b_router, w1, w3, w2):
    batch, seq, d = x_prompt.shape
    dec_batch, dec_seq, _ = x_sample.shape
    depth = w_mod.shape[0]
    h_diff = cache_k.shape[2]
    h_ret = state_ret.shape[3]
    n_exp = w1.shape[1]
    assert cache_k.shape[-1] == cache_v.shape[-1] == HEAD_W
    assert state_ret.shape[-2:] == (HEAD_W, HEAD_W)
    alpha = (2 * depth) ** 0.25

    n_cond = 8
    tok = _Tokens(batch, seq, dec_batch, dec_seq, n_cond)
    cs = jnp.concatenate([c_ctx[None], c, jnp.zeros((n_cond - 1 - dec_batch, d), F32)], 0)
    mod6 = _modulation(cs, w_mod, b_mod).reshape(depth * n_cond * 6, 1, d)

    w_in_b = w_in.astype(BF16)
    w_out_b = w_out.astype(BF16)
    w1b, w3b, w2b = w1.astype(BF16), w3.astype(BF16), w2.astype(BF16)
    ck_b = cache_k.astype(BF16)
    cv_b = jnp.concatenate([cache_v.astype(BF16), jnp.ones(cache_v.shape, BF16)], axis=-1)
    lamp = jnp.stack([lam_q1, lam_k1, lam_q2, lam_k2], axis=1)
    subln = subln_g.reshape(depth, 1, HEAD_W)
    ln_g4 = ln_g.reshape(depth, 2, 1, d)
    ln_b4 = ln_b.reshape(depth, 2, 1, d)
    w_router_hi = w_router.T.astype(BF16)
    w_router_t = jnp.stack([w_router_hi, (w_router.T - w_router_hi.astype(F32)).astype(BF16)])
    b_router_t = b_router.reshape(n_exp, 1)
    rope = _rope_tables(dec_seq)
    ret_tabs = _ret_tables(h_ret)

    x, h = _premod(x_prompt.reshape(tok.t_p, d), x_sample.reshape(tok.t_s, d), mod6, tok)
    ks_out, vs_out, ss_out = [], [], []
    for layer in range(depth):
        lam_init = 0.8 - 0.6 * math.exp(-0.3 * layer)
        z = _in_proj(h, w_in_b, layer)
        od_p, own_k, own_v = _attn_prompt(z, lamp, subln, layer, lam_init, batch, seq, h_diff)
        qb = _rope_q(z, rope, tok, h_diff)
        od_s = _attn_sample(qb, z, tok.t_p, rope, ck_b, cv_b, lamp, subln, layer, lam_init, dec_batch, dec_seq,
                            h_diff)
        or_p, s_new = _retention(z, ret_tabs, None, layer, batch, seq, 0, h_diff, h_ret, True)
        (or_s,) = _retention(z, ret_tabs, state_ret, layer, dec_batch, dec_seq, tok.t_p, h_diff, h_ret, False)
        ks_out.append(own_k)
        vs_out.append(own_v)
        ss_out.append(s_new)
        x1, h2, idx, gate = _out_proj(od_p, od_s, or_p, or_s, w_out_b, x, mod6, ln_g4, ln_b4,
                                      w_router_t, b_router_t, layer, tok, alpha)
        plan = _dispatch_plan(idx, gate, n_exp, tok.t)
        y2 = _experts(h2, *plan, w1b, w3b, w2b, layer, TOP_K * tok.t)
        last = layer == depth - 1
        outs = _ln2(x1, y2, mod6, ln_g4, ln_b4, layer, tok, alpha, last)
        x, h = outs
    xp = x.reshape(batch, seq, d)
    xs = h.reshape(dec_batch, dec_seq, d)
    return (xp, xs, jnp.stack(ks_out, axis=1), jnp.stack(vs_out, axis=1), jnp.stack(ss_out, axis=1))
```

```python
import functools
import math

import jax
import jax.numpy as jnp
from jax import lax
from jax.experimental import pallas as pl
from jax.experimental.pallas import tpu as pltpu

F32 = jnp.float32
BF16 = jnp.bfloat16

GRID_W = 64
ROPE_BASE = 10000.0
RET_CHUNK = 128
N_GROUPS = 4
TOP_K = 2
LN_EPS = 1e-5
HEAD_W = 128
MOE_ROWS = 256
VMEM_LIMIT = 56 * 1024 * 1024

_NT = (((1,), (1,)), ((), ()))
Q_SCALE = 0.125 * math.log2(math.e)


def _tile(n, pref, mult=8):
    t = min(n, pref)
    while n % t or (t % mult and t != n):
        t -= 1
    return t


def _params(sem, vmem=VMEM_LIMIT, flags=None):
    return pltpu.CompilerParams(dimension_semantics=sem, vmem_limit_bytes=vmem, flags=flags)


def _mod_kernel(c_ref, w_ref, b_ref, o_ref):
    c = c_ref[...]
    s = c * jax.nn.sigmoid(c)
    o_ref[...] = jnp.dot(s, w_ref[...], precision=lax.Precision.HIGHEST,
                         preferred_element_type=F32) + b_ref[...]


def _modulation(cs, w_mod, b_mod):
    depth, d, n = w_mod.shape
    rows = cs.shape[0]
    tn = _tile(n, 1024, 128)
    return pl.pallas_call(
        _mod_kernel,
        grid=(depth, n // tn),
        in_specs=[pl.BlockSpec((rows, d), lambda l, j: (0, 0)),
                  pl.BlockSpec((None, d, tn), lambda l, j: (l, 0, j)),
                  pl.BlockSpec((None, 1, tn), lambda l, j: (l, 0, j))],
        out_specs=pl.BlockSpec((None, rows, tn), lambda l, j: (l, 0, j)),
        out_shape=jax.ShapeDtypeStruct((depth, rows, n), F32),
        compiler_params=_params(("parallel", "parallel")),
        name="adaln_mod",
    )(cs, w_mod, b_mod.reshape(depth, 1, n))


class _Tokens:
    def __init__(self, batch, seq, dec_batch, dec_seq, n_cond):
        self.t_p = batch * seq
        self.t_s = dec_batch * dec_seq
        self.t = self.t_p + self.t_s
        self.dec_seq = dec_seq
        self.n_cond = n_cond

    def tile(self, pref):
        return _tile(math.gcd(self.t_p, self.dec_seq), pref)

    def group_spec(self, tm, w, prompt):
        n_p = self.t_p // tm
        n_s = self.t_s // tm
        if prompt:
            return pl.BlockSpec((tm, w), lambda i: (jnp.minimum(i, n_p - 1), 0))
        return pl.BlockSpec((tm, w), lambda i: (jnp.clip(i - n_p, 0, n_s - 1), 0))

    def mod_spec(self, layer, which, tm, d):
        t_p, dec_seq, n_cond = self.t_p, self.dec_seq, self.n_cond

        def idx(i, *_):
            r = i * tm
            cond = jnp.where(r < t_p, 0, 1 + (r - t_p) // dec_seq)
            return ((layer * n_cond + cond) * 6 + which, 0, 0)

        return pl.BlockSpec((None, 1, d), idx)


def _vec_spec(d):
    return pl.BlockSpec((1, d), lambda i, *_: (0, 0))


def _premod_kernel(xp_ref, xs_ref, sh_ref, sc_ref, x_ref, h_ref, *, n_p):
    x = jnp.where(pl.program_id(0) < n_p, xp_ref[...], xs_ref[...])
    x_ref[...] = x
    h_ref[...] = (x * (1.0 + sc_ref[...]) + sh_ref[...]).astype(BF16)


def _premod(x_p, x_s, mod6, tok):
    d = x_p.shape[1]
    tm = tok.tile(512)
    n_p = tok.t_p // tm
    row = pl.BlockSpec((tm, d), lambda i: (i, 0))
    return pl.pallas_call(
        functools.partial(_premod_kernel, n_p=n_p),
        grid=(tok.t // tm,),
        in_specs=[tok.group_spec(tm, d, True), tok.group_spec(tm, d, False),
                  tok.mod_spec(0, 0, tm, d), tok.mod_spec(0, 1, tm, d)],
        out_specs=[row, row],
        out_shape=[jax.ShapeDtypeStruct((tok.t, d), F32), jax.ShapeDtypeStruct((tok.t, d), BF16)],
        compiler_params=_params(("parallel",)),
        name="premod",
    )(x_p, x_s, mod6, mod6)


def _matmul_kernel(a_ref, w_ref, o_ref):
    o_ref[...] = jnp.dot(a_ref[...], w_ref[...], preferred_element_type=F32)


def _in_proj(h, w_in_b, layer):
    t, d = h.shape
    n = w_in_b.shape[-1]
    tm = _tile(t, 1024)
    tn = _tile(n, 1024, 128)
    return pl.pallas_call(
        _matmul_kernel,
        grid=(t // tm, n // tn),
        in_specs=[pl.BlockSpec((tm, d), lambda i, j: (i, 0)),
                  pl.BlockSpec((None, d, tn), lambda i, j: (layer, 0, j))],
        out_specs=pl.BlockSpec((tm, tn), lambda i, j: (i, j)),
        out_shape=jax.ShapeDtypeStruct((t, n), F32),
        compiler_params=_params(("parallel", "parallel")),
        name="in_proj",
    )(h, w_in_b)


def _rope(x, cos, sa, sb):
    return x * cos + pltpu.roll(x, HEAD_W - 16, 1) * sa + pltpu.roll(x, 16, 1) * sb


def _rope_q_kernel(z_ref, cos_ref, sa_ref, sb_ref, o_ref, *, n_heads):
    cos, sa, sb = cos_ref[...], sa_ref[...], sb_ref[...]
    for c in range(n_heads):
        cols = slice(c * HEAD_W, (c + 1) * HEAD_W)
        o_ref[:, cols] = (_rope(z_ref[:, cols], cos, sa, sb) * Q_SCALE).astype(BF16)


def _rope_q(z, tables, tok, h_diff):
    w = h_diff * HEAD_W
    tm = _tile(tok.dec_seq, 512)
    off = tok.t_p // tm
    nseq = tok.dec_seq // tm
    tab = pl.BlockSpec((tm, HEAD_W), lambda i: (i % nseq, 0))
    return pl.pallas_call(
        functools.partial(_rope_q_kernel, n_heads=h_diff),
        grid=(tok.t_s // tm,),
        in_specs=[pl.BlockSpec((tm, w), lambda i: (off + i, 0)), tab, tab, tab],
        out_specs=pl.BlockSpec((tm, w), lambda i: (i, 0)),
        out_shape=jax.ShapeDtypeStruct((tok.t_s, w), BF16),
        compiler_params=_params(("parallel",)),
        name="rope_q",
    )(z, *tables)


def _lam_value(lamp_ref, lam_init):
    lp = lamp_ref[...]
    a = jnp.sum(lp[0:1] * lp[1:2], axis=-1, keepdims=True)
    b = jnp.sum(lp[2:3] * lp[3:4], axis=-1, keepdims=True)
    return jnp.exp(a) - jnp.exp(b) + lam_init


def _softmax_pv(qc, chunks):
    m = acc = None
    for kk, vv in chunks:
        kk, vv = kk(), vv()
        s = lax.dot_general(qc, kk, _NT, preferred_element_type=F32)
        cm = jnp.max(s, axis=-1, keepdims=True)
        m_new = cm if m is None else jnp.maximum(m, cm)
        pv = jnp.dot(jnp.exp2(s - m_new).astype(BF16), vv, preferred_element_type=F32)
        acc = pv if m is None else jnp.exp2(m - m_new) * acc + pv
        m = m_new
    return acc[:, :HEAD_W] / acc[:, HEAD_W:]


def _q_halves(q):
    lane = lax.broadcasted_iota(jnp.int32, q.shape, 1)
    zero = jnp.zeros_like(q)
    return jnp.where(lane < HEAD_W // 2, q, zero), jnp.where(lane >= HEAD_W // 2, q, zero)


def _diff_norm(o1, o2, lam, g, lam_init):
    o = o1 - lam * o2
    o = o * lax.rsqrt(jnp.mean(o * o, axis=-1, keepdims=True) + LN_EPS) * g
    return o * (1.0 - lam_init)


def _diff_out(q, chunks, lam, g, lam_init):
    q1, q2 = _q_halves(q)
    return _diff_norm(_softmax_pv(q1, chunks), _softmax_pv(q2, chunks), lam, g, lam_init)


def _half_sq_norms(x):
    xf = x.astype(F32)
    sq = xf * xf
    lane = lax.broadcasted_iota(jnp.int32, sq.shape, 1)
    lo = jnp.sum(jnp.where(lane < HEAD_W // 2, sq, 0.0), axis=-1, keepdims=True)
    return lo, jnp.sum(sq, axis=-1, keepdims=True) - lo


def _softmax_pv_bounded(qc, bound, key_chunks, p_ref, vall_ref):
    col = 0
    for kk in key_chunks:
        kk = kk()
        s = lax.dot_general(qc, kk, _NT, preferred_element_type=F32)
        p_ref[:, col:col + kk.shape[0]] = jnp.exp2(s - bound).astype(BF16)
        col += kk.shape[0]
    acc = jnp.dot(p_ref[...], vall_ref[...], preferred_element_type=F32)
    return acc[:, :HEAD_W], acc[:, HEAD_W:]


def _attn_prompt_kernel(lamp_ref, q_ref, k_ref, v_ref, g_ref, ks_in, vs_in, o_ref, ok_ref, ov_ref, *, lam_init, hp):
    del ks_in, vs_in
    lam = _lam_value(lamp_ref, lam_init)
    g = g_ref[...]
    for j in range(hp):
        cols = slice(j * HEAD_W, (j + 1) * HEAD_W)
        k = k_ref[:, cols]
        v = v_ref[:, cols]
        ok_ref[j] = k
        ov_ref[j] = v
        q = (q_ref[:, cols] * Q_SCALE).astype(BF16)
        v1 = jnp.concatenate([v.astype(BF16), jnp.ones(v.shape, BF16)], axis=1)
        kb = k.astype(BF16)
        o = _diff_out(q, [(lambda kb=kb: kb, lambda v1=v1: v1)], lam, g, lam_init)
        o_ref[:, cols] = o.astype(BF16)


def _attn_prompt(z, lamp, subln_g, ks_all, vs_all, layer, lam_init, batch, seq, h_diff):
    t_p = batch * seq
    hp = _tile(h_diff, 4, 1)
    w = hp * HEAD_W
    ng = h_diff // hp
    blk = lambda c0: pl.BlockSpec((seq, w), lambda b, g: (b, c0 * ng + g))
    own = pl.BlockSpec((None, None, hp, seq, HEAD_W), lambda b, g: (b, layer, g, 0, 0))
    anywhere = pl.BlockSpec(memory_space=pl.ANY)
    return pl.pallas_call(
        functools.partial(_attn_prompt_kernel, lam_init=lam_init, hp=hp),
        grid=(batch, ng),
        in_specs=[pl.BlockSpec((None, 4, HEAD_W // 2), lambda b, g: (layer, 0, 0)),
                  blk(0), blk(1), blk(2),
                  pl.BlockSpec((None, 1, HEAD_W), lambda b, g: (layer, 0, 0)),
                  anywhere, anywhere],
        out_specs=[pl.BlockSpec((seq, w), lambda b, g: (b, g)), own, own],
        out_shape=[jax.ShapeDtypeStruct((t_p, h_diff * HEAD_W), BF16),
                   jax.ShapeDtypeStruct(ks_all.shape, F32), jax.ShapeDtypeStruct(vs_all.shape, F32)],
        input_output_aliases={5: 1, 6: 2},
        compiler_params=_params(("parallel", "parallel")),
        name="diff_attn_prompt",
    )(lamp, z, z, z, subln_g, ks_all, vs_all)


_MIN_ROW_SUM = 2.0 ** -64


def _attn_sample_kernel(lamp_ref, q_ref, k_ref, v_ref, ck_ref, cv_ref, g_ref, cos_ref, sa_ref, sb_ref,
                        o_ref, kmax_ref, kr_ref, vall_ref, p1_ref, p2_ref, *, lam_init, kc):
    n = k_ref.shape[0]
    past = ck_ref.shape[0]

    @pl.when(pl.program_id(2) == 0)
    def _():
        kr_ref[...] = _rope(k_ref[...], cos_ref[...], sa_ref[...], sb_ref[...]).astype(BF16)
        vall_ref[:past, :] = cv_ref[...]
        vall_ref[past:, :HEAD_W] = v_ref[...].astype(BF16)
        vall_ref[past:, HEAD_W:] = jnp.ones((n, HEAD_W), BF16)
        lo_c, hi_c = _half_sq_norms(ck_ref[...])
        lo_k, hi_k = _half_sq_norms(kr_ref[...])
        lo = jnp.maximum(jnp.max(lo_c, axis=0, keepdims=True), jnp.max(lo_k, axis=0, keepdims=True))
        hi = jnp.maximum(jnp.max(hi_c, axis=0, keepdims=True), jnp.max(hi_k, axis=0, keepdims=True))
        kmax_ref[0] = jnp.broadcast_to(jnp.sqrt(lo), kmax_ref.shape[1:])
        kmax_ref[1] = jnp.broadcast_to(jnp.sqrt(hi), kmax_ref.shape[1:])

    keys = [lambda: ck_ref[...]]
    vals = [lambda: vall_ref[:past, :]]
    for c in range(n // kc):
        keys.append(lambda c=c: kr_ref[c * kc:(c + 1) * kc, :])
        vals.append(lambda c=c: vall_ref[past + c * kc:past + (c + 1) * kc, :])
    lam = _lam_value(lamp_ref, lam_init)
    g = g_ref[...]
    q = q_ref[...]

    q1, q2 = _q_halves(q)
    qlo, qhi = _half_sq_norms(q)
    a1, l1 = _softmax_pv_bounded(q1, jnp.sqrt(qlo) * kmax_ref[0, 0:1, 0:1], keys, p1_ref, vall_ref)
    a2, l2 = _softmax_pv_bounded(q2, jnp.sqrt(qhi) * kmax_ref[1, 0:1, 0:1], keys, p2_ref, vall_ref)
    o_ref[...] = _diff_norm(a1 / l1, a2 / l2, lam, g, lam_init).astype(BF16)

    @pl.when(jnp.logical_not(jnp.min(jnp.minimum(l1, l2)) >= _MIN_ROW_SUM))
    def _():
        o_ref[...] = _diff_out(q, list(zip(keys, vals)), lam, g, lam_init).astype(BF16)


def _attn_sample(qb, z, row0, tables, ck, cv, lamp, subln_g, layer, lam_init, dec_batch, dec_seq, h_diff):
    assert row0 % dec_seq == 0
    tq = _tile(dec_seq, 1024)
    kc = _tile(dec_seq, 512)
    nq = dec_seq // tq
    s0 = row0 // dec_seq
    past = ck.shape[3]
    ctx = lambda w: pl.BlockSpec((None, None, None, past, w), lambda b, h, i: (b, layer, h, 0, 0))
    own = lambda g: pl.BlockSpec((dec_seq, HEAD_W), lambda b, h, i: (s0 + b, g * h_diff + h))
    tab_k = pl.BlockSpec((dec_seq, HEAD_W), lambda b, h, i: (0, 0))
    return pl.pallas_call(
        functools.partial(_attn_sample_kernel, lam_init=lam_init, kc=kc),
        grid=(dec_batch, h_diff, nq),
        in_specs=[pl.BlockSpec((None, 4, HEAD_W // 2), lambda b, h, i: (layer, 0, 0)),
                  pl.BlockSpec((tq, HEAD_W), lambda b, h, i: (b * nq + i, h)),
                  own(1), own(2), ctx(HEAD_W), ctx(2 * HEAD_W),
                  pl.BlockSpec((None, 1, HEAD_W), lambda b, h, i: (layer, 0, 0)),
                  tab_k, tab_k, tab_k],
        out_specs=pl.BlockSpec((tq, HEAD_W), lambda b, h, i: (b * nq + i, h)),
        out_shape=jax.ShapeDtypeStruct((dec_batch * dec_seq, h_diff * HEAD_W), BF16),
        scratch_shapes=[pltpu.VMEM((2, 8, HEAD_W), F32), pltpu.VMEM((dec_seq, HEAD_W), BF16),
                        pltpu.VMEM((past + dec_seq, 2 * HEAD_W), BF16),
                        pltpu.VMEM((tq, past + dec_seq), BF16), pltpu.VMEM((tq, past + dec_seq), BF16)],
        compiler_params=_params(("parallel", "parallel", "arbitrary")),
        name="diff_attn_sample",
    )(lamp, qb, z, z, ck, cv, subln_g, *tables)


def _ret_kernel(*refs, has_state, emit_state, nc, scale, hp):
    q_ref, k_ref, v_ref, g_ref, if_ref, ib_ref, cols_ref = refs[:7]
    refs = refs[7:]
    if has_state:
        s0_ref, refs = refs[0], refs[1:]
    if emit_state:
        refs = refs[1:]
    o_ref, refs = refs[0], refs[1:]
    if emit_state:
        s_ref, refs = refs[0], refs[1:]
    of_ref, ob_ref, sf_ref, sb_ref = refs
    c_ = RET_CHUNK
    unroll = _tile(nc, 4, 1)

    def chunk_rows(c):
        return pl.ds(pl.multiple_of(c * c_, c_), c_)

    for j in range(hp):
        lanes = slice(j * HEAD_W, (j + 1) * HEAD_W)
        intra_f = if_ref[j]
        intra_b = ib_ref[j]
        cols = cols_ref[j]
        qdf, kdf, qdb, kdb = cols[:, 0:1], cols[:, 1:2], cols[:, 2:3], cols[:, 3:4]
        cdf, cdb = cols[0:1, 4:5], cols[0:1, 5:6]

        def intra_and_state(c, s, intra, kd, cd, out_ref, st_ref):
            rows = chunk_rows(c)
            st_ref[c] = s.astype(BF16)
            q = q_ref[rows, lanes].astype(BF16)
            k = k_ref[rows, lanes] * scale
            v = v_ref[rows, lanes].astype(BF16)
            a = lax.dot_general(q, k.astype(BF16), _NT, preferred_element_type=F32) * intra
            out_ref[rows, :] = jnp.dot(a.astype(BF16), v, preferred_element_type=F32)
            kt = jnp.transpose(k * kd).astype(BF16)
            return cd * s + jnp.dot(kt, v, preferred_element_type=F32)

        def scan(i, carry):
            sf, sb = carry
            sf = intra_and_state(i, sf, intra_f, kdf, cdf, of_ref, sf_ref)
            sb = intra_and_state(nc - 1 - i, sb, intra_b, kdb, cdb, ob_ref, sb_ref)
            return sf, sb

        if has_state:
            init = (s0_ref[0, j], s0_ref[1, j])
        else:
            init = (jnp.zeros((HEAD_W, HEAD_W), F32), jnp.zeros((HEAD_W, HEAD_W), F32))
        sf, sb = lax.fori_loop(0, nc, scan, init, unroll=unroll)
        if emit_state:
            s_ref[0, j] = sf
            s_ref[1, j] = sb

        def finish(c, carry):
            rows = chunk_rows(c)
            q = q_ref[rows, lanes]
            o = of_ref[rows, :] + ob_ref[rows, :]
            o = o + jnp.dot((q * qdf).astype(BF16), sf_ref[c], preferred_element_type=F32)
            o = o + jnp.dot((q * qdb).astype(BF16), sb_ref[c], preferred_element_type=F32)
            mu = jnp.mean(o, axis=-1, keepdims=True)
            oc = o - mu
            o = oc * lax.rsqrt(jnp.mean(oc * oc, axis=-1, keepdims=True) + LN_EPS)
            g = g_ref[rows, lanes]
            o_ref[rows, lanes] = (o * (g * jax.nn.sigmoid(g))).astype(BF16)
            return carry

        lax.fori_loop(0, nc, finish, 0, unroll=unroll)


def _retention(z, tabs, state, layer, n_seq, seq, row0, h_diff, h_ret, states_all=None):
    emit_state = states_all is not None
    intra_f, intra_b, cols = tabs
    c_ = RET_CHUNK
    nc = seq // c_
    assert row0 % seq == 0
    off = row0 // seq
    hp = _tile(h_ret, 4, 1) if nc <= 4 else 1
    w = hp * HEAD_W
    ng = h_ret // hp
    assert (3 * h_diff) % hp == 0
    c0 = 3 * h_diff // hp
    blk = lambda q: pl.BlockSpec((seq, w), lambda b, g: (off + b, c0 + q * ng + g))
    htab = lambda shape: pl.BlockSpec((hp,) + shape, lambda b, g: (g,) + (0,) * len(shape))
    in_specs = [blk(0), blk(1), blk(2), blk(3), htab((c_, c_)), htab((c_, c_)), htab((c_, 8))]
    args = [z, z, z, z, intra_f, intra_b, cols]
    if state is not None:
        in_specs.append(pl.BlockSpec((None, None, 2, hp, HEAD_W, HEAD_W), lambda b, g: (b, layer, 0, g, 0, 0)))
        args.append(state)
    out_specs = [pl.BlockSpec((seq, w), lambda b, g: (b, g))]
    out_shape = [jax.ShapeDtypeStruct((n_seq * seq, h_ret * HEAD_W), BF16)]
    aliases = {}
    if emit_state:
        aliases = {len(args): 1}
        in_specs.append(pl.BlockSpec(memory_space=pl.ANY))
        args.append(states_all)
        out_specs.append(pl.BlockSpec((None, None, 2, hp, HEAD_W, HEAD_W), lambda b, g: (b, layer, 0, g, 0, 0)))
        out_shape.append(jax.ShapeDtypeStruct(states_all.shape, F32))

    return pl.pallas_call(
        functools.partial(_ret_kernel, has_state=state is not None, emit_state=emit_state, nc=nc,
                          scale=HEAD_W ** -0.5, hp=hp),
        grid=(n_seq, ng),
        in_specs=in_specs,
        out_specs=out_specs,
        out_shape=out_shape,
        scratch_shapes=[pltpu.VMEM((seq, HEAD_W), F32), pltpu.VMEM((seq, HEAD_W), F32),
                        pltpu.VMEM((nc, HEAD_W, HEAD_W), BF16), pltpu.VMEM((nc, HEAD_W, HEAD_W), BF16)],
        input_output_aliases=aliases,
        compiler_params=_params(("parallel", "parallel")),
        name="retention_%d" % seq,
    )(*args)


def _layer_norm(v, g, b):
    mu = jnp.mean(v, axis=-1, keepdims=True)
    vc = v - mu
    var = jnp.mean(vc * vc, axis=-1, keepdims=True)
    return vc * lax.rsqrt(var + LN_EPS) * g + b


def _route(p, n_exp):
    per = n_exp // N_GROUPS
    rows = [p[e:e + 1, :] for e in range(n_exp)]
    scores = []
    for g in range(N_GROUPS):
        a, b, c, d = rows[per * g:per * g + per]
        hi1, lo1, hi2, lo2 = jnp.maximum(a, b), jnp.minimum(a, b), jnp.maximum(c, d), jnp.minimum(c, d)
        scores.append(jnp.maximum(hi1, hi2) + jnp.maximum(jnp.minimum(hi1, hi2), jnp.maximum(lo1, lo2)))
    best = scores[0]
    gsel = jnp.zeros(best.shape, jnp.int32)
    for g in range(1, N_GROUPS):
        better = scores[g] > best
        best = jnp.where(better, scores[g], best)
        gsel = jnp.where(better, g, gsel)
    vals = []
    for j in range(per):
        v = rows[(N_GROUPS - 1) * per + j]
        for g in range(N_GROUPS - 2, -1, -1):
            v = jnp.where(gsel == g, rows[g * per + j], v)
        vals.append(v)
    b1, i1 = vals[0], jnp.zeros(best.shape, jnp.int32)
    for j in range(1, per):
        gt = vals[j] > b1
        b1 = jnp.where(gt, vals[j], b1)
        i1 = jnp.where(gt, j, i1)
    b2, i2 = jnp.full(best.shape, -1.0, F32), jnp.zeros(best.shape, jnp.int32)
    for j in range(per):
        gt = jnp.where(i1 == j, -2.0, vals[j]) > b2
        b2 = jnp.where(gt, vals[j], b2)
        i2 = jnp.where(gt, j, i2)
    tot = b1 + b2
    return gsel * per + i1, gsel * per + i2, b1 / tot, b2 / tot


def _outproj_kernel(odp_ref, ods_ref, orp_ref, ors_ref, wd_ref, wr_ref, x_ref, ga_ref, shf_ref, scf_ref,
                    lng_ref, lnb_ref, wrt_ref, brt_ref, x1_ref, h2_ref, idx_ref, gate_ref, *, alpha, n_p):
    prompt = pl.program_id(0) < n_p
    o_d = jnp.where(prompt, odp_ref[...], ods_ref[...])
    o_r = jnp.where(prompt, orp_ref[...], ors_ref[...])
    mix = jnp.dot(o_d, wd_ref[...], preferred_element_type=F32)
    mix = mix + jnp.dot(o_r, wr_ref[...], preferred_element_type=F32)
    x1 = _layer_norm(alpha * x_ref[...] + ga_ref[...] * mix, lng_ref[...], lnb_ref[...])
    x1_ref[...] = x1
    h2 = x1 * (1.0 + scf_ref[...]) + shf_ref[...]
    h2_ref[...] = h2
    h_hi = h2.astype(BF16)
    h_lo = (h2 - h_hi.astype(F32)).astype(BF16)
    w_hi, w_lo = wrt_ref[0], wrt_ref[1]
    logits = (lax.dot_general(w_hi, h_hi, _NT, preferred_element_type=F32)
              + lax.dot_general(w_lo, h_hi, _NT, preferred_element_type=F32)
              + lax.dot_general(w_hi, h_lo, _NT, preferred_element_type=F32)) + brt_ref[...]
    e = jnp.exp(logits - jnp.max(logits, axis=0, keepdims=True))
    p = e / jnp.sum(e, axis=0, keepdims=True)
    e1, e2, g1, g2 = _route(p, logits.shape[0])
    idx_ref[0:1, :] = e1
    idx_ref[1:2, :] = e2
    gate_ref[0:1, :] = g1
    gate_ref[1:2, :] = g2


def _out_proj(od_p, od_s, or_p, or_s, w_out_b, x, mod6, ln_g, ln_b, w_router_t, b_router_t, layer, tok, alpha):
    t, d = x.shape
    wd = od_p.shape[1]
    wr = or_p.shape[1]
    n_exp = w_router_t.shape[1]
    tm = tok.tile(512)
    row = lambda w: pl.BlockSpec((tm, w), lambda i: (i, 0))
    lnv = pl.BlockSpec((None, None, 1, d), lambda i: (layer, 0, 0, 0))
    return pl.pallas_call(
        functools.partial(_outproj_kernel, alpha=alpha, n_p=tok.t_p // tm),
        grid=(t // tm,),
        in_specs=[tok.group_spec(tm, wd, True), tok.group_spec(tm, wd, False),
                  tok.group_spec(tm, wr, True), tok.group_spec(tm, wr, False),
                  pl.BlockSpec((None, wd, d), lambda i: (layer, 0, 0)),
                  pl.BlockSpec((None, wr, d), lambda i: (layer, wd // wr, 0)),
                  row(d),
                  tok.mod_spec(layer, 2, tm, d), tok.mod_spec(layer, 3, tm, d), tok.mod_spec(layer, 4, tm, d),
                  lnv, lnv,
                  pl.BlockSpec((2, n_exp, d), lambda i: (0, 0, 0)),
                  pl.BlockSpec((n_exp, 1), lambda i: (0, 0))],
        out_specs=[row(d), row(d),
                   pl.BlockSpec((TOP_K, tm), lambda i: (0, i)),
                   pl.BlockSpec((TOP_K, tm), lambda i: (0, i))],
        out_shape=[jax.ShapeDtypeStruct((t, d), F32), jax.ShapeDtypeStruct((t, d), F32),
                   jax.ShapeDtypeStruct((TOP_K, t), jnp.int32), jax.ShapeDtypeStruct((TOP_K, t), F32)],
        compiler_params=_params(("parallel",)),
        name="out_proj_ln_router",
    )(od_p, od_s, or_p, or_s, w_out_b, w_out_b, x, mod6, mod6, mod6, ln_g, ln_b, w_router_t, b_router_t)


def _moe_kernel(be_ref, cnt_ref, tok0_ref, tokn_ref, dst_ref, gate_ref, x_hbm, w1_ref, w3_ref, w2_ref, y_hbm,
                xbuf, ybuf, gsem, ssem, *, n_out):
    del be_ref
    s = pl.program_id(0)
    rows = xbuf.shape[1]
    c_old, c_cur, c_next = cnt_ref[s], cnt_ref[s + 2], cnt_ref[s + 3]

    def gather_copy(tok_ref, r, p):
        return pltpu.make_async_copy(x_hbm.at[pl.ds(tok_ref[0, 0, r], 1), :], xbuf.at[p, pl.ds(r, 1), :], gsem.at[p])

    def scatter_copy(r, p):
        return pltpu.make_async_copy(ybuf.at[p, pl.ds(r, 1), :], y_hbm.at[pl.ds(dst_ref[0, 0, r], 1), :], ssem.at[p])

    @pl.when(s == 0)
    def _():
        for p in range(2):
            ybuf[p] = jnp.zeros(ybuf.shape[1:], F32)
            spare = pltpu.make_async_copy(ybuf.at[p], y_hbm.at[pl.ds(n_out + p * rows, rows), :], ssem.at[p])
            spare.start()
            spare.wait()

        def first(r, c):
            gather_copy(tok0_ref, r, 0).start()
            return c

        lax.fori_loop(0, rows, first, 0)

    for p in range(2):
        mine = s % 2 == p

        @pl.when(jnp.logical_and(mine, c_old > 0))
        def _():
            pltpu.make_async_copy(ybuf.at[p], y_hbm.at[pl.ds(0, rows), :], ssem.at[p]).wait()

        @pl.when(jnp.logical_and(mine, c_next > 0))
        def _():
            for r in range(rows):
                gather_copy(tokn_ref, r, 1 - p).start()

        @pl.when(jnp.logical_and(mine, c_cur > 0))
        def _():
            pltpu.make_async_copy(x_hbm.at[pl.ds(0, rows), :], xbuf.at[p], gsem.at[p]).wait()
            x = xbuf[p].astype(BF16)
            a = jnp.dot(x, w1_ref[...], preferred_element_type=F32)
            g = jnp.dot(x, w3_ref[...], preferred_element_type=F32)
            h = (a * jax.nn.sigmoid(a) * g).astype(BF16)
            ybuf[p] = jnp.dot(h, w2_ref[...], preferred_element_type=F32) * gate_ref[...]
            for r in range(rows):
                scatter_copy(r, p).start()


def _experts(h2, block_e, block_cnt, slot_tok, slot_dst, slot_gate, w1b, w3b, w2b, layer, n_out):
    t, d = h2.shape
    nb = block_e.shape[0]
    ff = w1b.shape[-1]
    rows = MOE_ROWS
    last = nb - 1
    cnt_pad = jnp.concatenate([jnp.zeros((2,), jnp.int32), block_cnt, jnp.zeros((3,), jnp.int32)])
    smem = lambda off: pl.BlockSpec((1, 1, rows), lambda s, be, n: (jnp.minimum(s + off, last), 0, 0),
                                    memory_space=pltpu.SMEM)
    wspec = lambda shape: pl.BlockSpec((None, None) + shape,
                                       lambda s, be, n: (layer, be[jnp.minimum(s, last)], 0, 0))
    grid_spec = pltpu.PrefetchScalarGridSpec(
        num_scalar_prefetch=2,
        grid=(nb + 2,),
        in_specs=[smem(0), smem(1), smem(0),
                  pl.BlockSpec((rows, 1), lambda s, be, n: (jnp.minimum(s, last), 0)),
                  pl.BlockSpec(memory_space=pl.ANY),
                  wspec((d, ff)), wspec((d, ff)), wspec((ff, d))],
        out_specs=pl.BlockSpec(memory_space=pl.ANY),
        scratch_shapes=[pltpu.VMEM((2, rows, d), F32), pltpu.VMEM((2, rows, d), F32),
                        pltpu.SemaphoreType.DMA((2,)), pltpu.SemaphoreType.DMA((2,))])
    tok3 = slot_tok.reshape(nb, 1, rows)
    return pl.pallas_call(
        functools.partial(_moe_kernel, n_out=n_out),
        grid_spec=grid_spec,
        out_shape=jax.ShapeDtypeStruct((n_out + 2 * rows, d), F32),
        compiler_params=_params(("arbitrary",)),
        name="moe_experts",
    )(block_e, cnt_pad, tok3, tok3, slot_dst.reshape(nb, 1, rows), slot_gate.reshape(nb * rows, 1),
      h2, w1b, w3b, w2b)


def _dispatch_plan(idx, gate, n_exp, t):
    rows = MOE_ROWS
    a = t * TOP_K
    flat_e = idx.T.reshape(a)
    flat_g = gate.T.reshape(a)
    _, order = lax.sort((flat_e, jnp.arange(a, dtype=jnp.int32)), num_keys=1, is_stable=True)
    experts = jnp.arange(n_exp, dtype=jnp.int32)
    counts = jnp.sum((flat_e[None, :] == experts[:, None]).astype(jnp.int32), axis=1)
    padded = (counts + rows - 1) // rows * rows
    ends_p = jnp.cumsum(padded)
    starts_p = ends_p - padded
    starts = jnp.cumsum(counts) - counts
    nb = -(-a // rows) + n_exp
    blk = jnp.arange(nb, dtype=jnp.int32)
    block_e = jnp.minimum(jnp.sum((blk[:, None] * rows >= ends_p[None, :]).astype(jnp.int32), axis=1), n_exp - 1)
    onehot = block_e[:, None] == experts[None, :]
    pick = lambda tab: jnp.sum(jnp.where(onehot, tab[None, :], 0), axis=1)
    within = (blk * rows - pick(starts_p))[:, None] + jnp.arange(rows, dtype=jnp.int32)[None, :]
    valid = within < pick(counts)[:, None]
    src = order[jnp.clip(pick(starts)[:, None] + within, 0, a - 1)]
    slot_tok = jnp.where(valid, src // TOP_K, 0)
    spare = a + (blk % 2)[:, None] * rows + jnp.arange(rows, dtype=jnp.int32)[None, :]
    slot_dst = jnp.where(valid, (src % TOP_K) * t + src // TOP_K, spare)
    slot_gate = jnp.where(valid, flat_g[src], 0.0)
    block_cnt = jnp.sum(valid.astype(jnp.int32), axis=1)
    return block_e, block_cnt, slot_tok, slot_dst, slot_gate


def _ln2_kernel(*refs, alpha, emit_h, n_p):
    x1_ref, y0_ref, y1_ref, gf_ref, lng_ref, lnb_ref = refs[:6]
    y = y0_ref[...] + y1_ref[...]
    x2 = _layer_norm(alpha * x1_ref[...] + gf_ref[...] * y, lng_ref[...], lnb_ref[...])
    if emit_h:
        sh_ref, sc_ref, x2_ref, h_ref = refs[6:]
        h_ref[...] = (x2 * (1.0 + sc_ref[...]) + sh_ref[...]).astype(BF16)
        x2_ref[...] = x2
    else:
        xp_ref, xs_ref = refs[6:]

        @pl.when(pl.program_id(0) < n_p)
        def _():
            xp_ref[...] = x2

        @pl.when(pl.program_id(0) >= n_p)
        def _():
            xs_ref[...] = x2


def _ln2(x1, y2, mod6, ln_g, ln_b, layer, tok, alpha, last):
    t, d = x1.shape
    tm = tok.tile(512)
    nt = t // tm
    row = lambda off: pl.BlockSpec((tm, d), lambda i: (off + i, 0))
    lnv = pl.BlockSpec((None, None, 1, d), lambda i: (layer, 1, 0, 0))
    in_specs = [row(0), row(0), row(nt), tok.mod_spec(layer, 5, tm, d), lnv, lnv]
    args = [x1, y2, y2, mod6, ln_g, ln_b]
    if last:
        out_specs = [tok.group_spec(tm, d, True), tok.group_spec(tm, d, False)]
        out_shape = [jax.ShapeDtypeStruct((tok.t_p, d), F32), jax.ShapeDtypeStruct((tok.t_s, d), F32)]
    else:
        in_specs += [tok.mod_spec(layer + 1, 0, tm, d), tok.mod_spec(layer + 1, 1, tm, d)]
        args += [mod6, mod6]
        out_specs = [row(0), row(0)]
        out_shape = [jax.ShapeDtypeStruct((t, d), F32), jax.ShapeDtypeStruct((t, d), BF16)]
    return pl.pallas_call(
        functools.partial(_ln2_kernel, alpha=alpha, emit_h=not last, n_p=tok.t_p // tm),
        grid=(nt,),
        in_specs=in_specs,
        out_specs=out_specs,
        out_shape=out_shape,
        compiler_params=_params(("arbitrary",)),
        name="ln2_mod",
    )(*args)


def _rope_tables(n_tok):
    rows = n_tok // GRID_W
    axis = HEAD_W // 4
    r = jnp.repeat(jnp.arange(rows, dtype=F32), GRID_W)
    col = jnp.tile(jnp.arange(GRID_W, dtype=F32), rows)
    inv = ROPE_BASE ** (-jnp.arange(0, axis, 2, dtype=F32) / axis)
    ar = r[:, None] * inv[None]
    ac = col[:, None] * inv[None]
    ang = jnp.concatenate([ar, ar, ac, ac], -1)
    cos, sin = jnp.cos(ang), jnp.sin(ang)
    cos, sin = jnp.tile(cos, (1, 2)), jnp.tile(sin, (1, 2))
    low = (jnp.arange(HEAD_W) % (axis)) < axis // 2
    return cos, jnp.where(low, -sin, 0.0), jnp.where(low, 0.0, sin)


def _ret_tables(h_ret):
    c_ = RET_CHUNK
    h = jnp.arange(h_ret, dtype=F32)
    lg_f = jnp.log(1.0 - jnp.exp2(-5.0 - h))
    lg_b = jnp.log(1.0 - jnp.exp2(-5.5 - h))
    pos = jnp.arange(c_, dtype=F32)
    dist = pos[:, None] - pos[None, :]
    intra_f = jnp.where(dist >= 0, jnp.exp(lg_f[:, None, None] * jnp.maximum(dist, 0.0)), 0.0)
    intra_b = jnp.where(dist <= 0, jnp.exp(lg_b[:, None, None] * jnp.maximum(-dist, 0.0)), 0.0)
    qd_f = jnp.exp(lg_f[:, None] * (pos + 1.0))
    kd_f = jnp.exp(lg_f[:, None] * (c_ - 1.0 - pos))
    qd_b = jnp.exp(lg_b[:, None] * (c_ - pos))
    kd_b = jnp.exp(lg_b[:, None] * pos)
    cd_f = jnp.broadcast_to(jnp.exp(lg_f * c_)[:, None], (h_ret, c_))
    cd_b = jnp.broadcast_to(jnp.exp(lg_b * c_)[:, None], (h_ret, c_))
    zero = jnp.zeros((h_ret, c_), F32)
    cols = jnp.stack([qd_f, kd_f, qd_b, kd_b, cd_f, cd_b, zero, zero], axis=2)
    return intra_f, intra_b, cols


def kernel(x_prompt, x_sample, c, cache_k, cache_v, state_ret, c_ctx, w_mod, b_mod, w_in, w_out,
           lam_q1, lam_k1, lam_q2, lam_k2, subln_g, ln_g, ln_b, w_router, b_router, w1, w3, w2):
    batch, seq, d = x_prompt.shape
    dec_batch, dec_seq, _ = x_sample.shape
    depth = w_mod.shape[0]
    h_diff = cache_k.shape[2]
    h_ret = state_ret.shape[3]
    n_exp = w1.shape[1]
    assert cache_k.shape[-1] == cache_v.shape[-1] == HEAD_W
    assert state_ret.shape[-2:] == (HEAD_W, HEAD_W)
    alpha = (2 * depth) ** 0.25

    n_cond = 8
    tok = _Tokens(batch, seq, dec_batch, dec_seq, n_cond)
    cs = jnp.concatenate([c_ctx[None], c, jnp.zeros((n_cond - 1 - dec_batch, d), F32)], 0)
    mod6 = _modulation(cs, w_mod, b_mod).reshape(depth * n_cond * 6, 1, d)

    w_in_b = w_in.astype(BF16)
    w_out_b = w_out.astype(BF16)
    w1b, w3b, w2b = w1.astype(BF16), w3.astype(BF16), w2.astype(BF16)
    ck_b = cache_k.astype(BF16)
    cv_b = jnp.concatenate([cache_v.astype(BF16), jnp.ones(cache_v.shape, BF16)], axis=-1)
    lamp = jnp.stack([lam_q1, lam_k1, lam_q2, lam_k2], axis=1)
    subln = subln_g.reshape(depth, 1, HEAD_W)
    ln_g4 = ln_g.reshape(depth, 2, 1, d)
    ln_b4 = ln_b.reshape(depth, 2, 1, d)
    w_router_hi = w_router.T.astype(BF16)
    w_router_t = jnp.stack([w_router_hi, (w_router.T - w_router_hi.astype(F32)).astype(BF16)])
    b_router_t = b_router.reshape(n_exp, 1)
    rope = _rope_tables(dec_seq)
    ret_tabs = _ret_tables(h_ret)

    x, h = _premod(x_prompt.reshape(tok.t_p, d), x_sample.reshape(tok.t_s, d), mod6, tok)
    seq_heads = (batch, depth, h_diff, seq, HEAD_W)
    ks_all, vs_all = jnp.zeros(seq_heads, F32), jnp.zeros(seq_heads, F32)
    ss_all = jnp.zeros((batch, depth, 2, h_ret, HEAD_W, HEAD_W), F32)
    for layer in range(depth):
        lam_init = 0.8 - 0.6 * math.exp(-0.3 * layer)
        z = _in_proj(h, w_in_b, layer)
        od_p, ks_all, vs_all = _attn_prompt(z, lamp, subln, ks_all, vs_all, layer, lam_init, batch, seq, h_diff)
        qb = _rope_q(z, rope, tok, h_diff)
        od_s = _attn_sample(qb, z, tok.t_p, rope, ck_b, cv_b, lamp, subln, layer, lam_init, dec_batch, dec_seq,
                            h_diff)
        or_p, ss_all = _retention(z, ret_tabs, None, layer, batch, seq, 0, h_diff, h_ret, ss_all)
        (or_s,) = _retention(z, ret_tabs, state_ret, layer, dec_batch, dec_seq, tok.t_p, h_diff, h_ret)
        x1, h2, idx, gate = _out_proj(od_p, od_s, or_p, or_s, w_out_b, x, mod6, ln_g4, ln_b4,
                                      w_router_t, b_router_t, layer, tok, alpha)
        plan = _dispatch_plan(idx, gate, n_exp, tok.t)
        y2 = _experts(h2, *plan, w1b, w3b, w2b, layer, TOP_K * tok.t)
        last = layer == depth - 1
        outs = _ln2(x1, y2, mod6, ln_g4, ln_b4, layer, tok, alpha, last)
        x, h = outs
    xp = x.reshape(batch, seq, d)
    xs = h.reshape(dec_batch, dec_seq, d)
    return (xp, xs, ks_all, vs_all, ss_all)
```

```python
import functools
import math

import jax
import jax.numpy as jnp
from jax import lax
from jax.experimental import pallas as pl
from jax.experimental.pallas import tpu as pltpu

F32 = jnp.float32
BF16 = jnp.bfloat16

GRID_W = 64
ROPE_BASE = 10000.0
RET_CHUNK = 128
N_GROUPS = 4
TOP_K = 2
LN_EPS = 1e-5
HEAD_W = 128
MOE_ROWS = 256
VMEM_LIMIT = 56 * 1024 * 1024

_NT = (((1,), (1,)), ((), ()))
Q_SCALE = 0.125 * math.log2(math.e)


def _tile(n, pref, mult=8):
    t = min(n, pref)
    while n % t or (t % mult and t != n):
        t -= 1
    return t


def _params(sem, vmem=VMEM_LIMIT, flags=None):
    return pltpu.CompilerParams(dimension_semantics=sem, vmem_limit_bytes=vmem, flags=flags)


def _mod_kernel(c_ref, w_ref, b_ref, o_ref):
    c = c_ref[...]
    s = c * jax.nn.sigmoid(c)
    o_ref[...] = jnp.dot(s, w_ref[...], precision=lax.Precision.HIGHEST,
                         preferred_element_type=F32) + b_ref[...]


def _modulation(cs, w_mod, b_mod):
    depth, d, n = w_mod.shape
    rows = cs.shape[0]
    tn = _tile(n, 1024, 128)
    return pl.pallas_call(
        _mod_kernel,
        grid=(depth, n // tn),
        in_specs=[pl.BlockSpec((rows, d), lambda l, j: (0, 0)),
                  pl.BlockSpec((None, d, tn), lambda l, j: (l, 0, j)),
                  pl.BlockSpec((None, 1, tn), lambda l, j: (l, 0, j))],
        out_specs=pl.BlockSpec((None, rows, tn), lambda l, j: (l, 0, j)),
        out_shape=jax.ShapeDtypeStruct((depth, rows, n), F32),
        compiler_params=_params(("parallel", "parallel")),
        name="adaln_mod",
    )(cs, w_mod, b_mod.reshape(depth, 1, n))


class _Tokens:
    def __init__(self, batch, seq, dec_batch, dec_seq, n_cond):
        self.t_p = batch * seq
        self.t_s = dec_batch * dec_seq
        self.t = self.t_p + self.t_s
        self.dec_seq = dec_seq
        self.n_cond = n_cond

    def tile(self, pref):
        return _tile(math.gcd(self.t_p, self.dec_seq), pref)

    def group_spec(self, tm, w, prompt):
        n_p = self.t_p // tm
        n_s = self.t_s // tm
        if prompt:
            return pl.BlockSpec((tm, w), lambda i: (jnp.minimum(i, n_p - 1), 0))
        return pl.BlockSpec((tm, w), lambda i: (jnp.clip(i - n_p, 0, n_s - 1), 0))

    def mod_spec(self, layer, which, tm, d):
        t_p, dec_seq, n_cond = self.t_p, self.dec_seq, self.n_cond

        def idx(i, *_):
            r = i * tm
            cond = jnp.where(r < t_p, 0, 1 + (r - t_p) // dec_seq)
            return ((layer * n_cond + cond) * 6 + which, 0, 0)

        return pl.BlockSpec((None, 1, d), idx)


def _premod_kernel(xp_ref, xs_ref, sh_ref, sc_ref, x_ref, h_ref, *, n_p):
    x = jnp.where(pl.program_id(0) < n_p, xp_ref[...], xs_ref[...])
    x_ref[...] = x
    h_ref[...] = (x * (1.0 + sc_ref[...]) + sh_ref[...]).astype(BF16)


def _premod(x_p, x_s, mod6, tok):
    d = x_p.shape[1]
    tm = tok.tile(512)
    n_p = tok.t_p // tm
    row = pl.BlockSpec((tm, d), lambda i: (i, 0))
    return pl.pallas_call(
        functools.partial(_premod_kernel, n_p=n_p),
        grid=(tok.t // tm,),
        in_specs=[tok.group_spec(tm, d, True), tok.group_spec(tm, d, False),
                  tok.mod_spec(0, 0, tm, d), tok.mod_spec(0, 1, tm, d)],
        out_specs=[row, row],
        out_shape=[jax.ShapeDtypeStruct((tok.t, d), F32), jax.ShapeDtypeStruct((tok.t, d), BF16)],
        compiler_params=_params(("parallel",)),
        name="premod",
    )(x_p, x_s, mod6, mod6)


def _matmul_kernel(a_ref, w_ref, o_ref):
    o_ref[...] = jnp.dot(a_ref[...], w_ref[...], preferred_element_type=F32)


def _in_proj(h, w_in_b, layer):
    t, d = h.shape
    n = w_in_b.shape[-1]
    tm = _tile(t, 1024)
    tn = _tile(n, 1024, 128)
    return pl.pallas_call(
        _matmul_kernel,
        grid=(t // tm, n // tn),
        in_specs=[pl.BlockSpec((tm, d), lambda i, j: (i, 0)),
                  pl.BlockSpec((None, d, tn), lambda i, j: (layer, 0, j))],
        out_specs=pl.BlockSpec((tm, tn), lambda i, j: (i, j)),
        out_shape=jax.ShapeDtypeStruct((t, n), F32),
        compiler_params=_params(("parallel", "parallel")),
        name="in_proj",
    )(h, w_in_b)


def _rope(x, cos, sa, sb):
    return x * cos + pltpu.roll(x, HEAD_W - 16, 1) * sa + pltpu.roll(x, 16, 1) * sb


def _rope_q_kernel(z_ref, cos_ref, sa_ref, sb_ref, o_ref, *, n_heads):
    cos, sa, sb = cos_ref[...], sa_ref[...], sb_ref[...]
    for c in range(n_heads):
        cols = slice(c * HEAD_W, (c + 1) * HEAD_W)
        o_ref[:, cols] = (_rope(z_ref[:, cols], cos, sa, sb) * Q_SCALE).astype(BF16)


def _rope_q(z, tables, tok, h_diff):
    w = h_diff * HEAD_W
    tm = _tile(tok.dec_seq, 512)
    off = tok.t_p // tm
    nseq = tok.dec_seq // tm
    tab = pl.BlockSpec((tm, HEAD_W), lambda i: (i % nseq, 0))
    return pl.pallas_call(
        functools.partial(_rope_q_kernel, n_heads=h_diff),
        grid=(tok.t_s // tm,),
        in_specs=[pl.BlockSpec((tm, w), lambda i: (off + i, 0)), tab, tab, tab],
        out_specs=pl.BlockSpec((tm, w), lambda i: (i, 0)),
        out_shape=jax.ShapeDtypeStruct((tok.t_s, w), BF16),
        compiler_params=_params(("parallel",)),
        name="rope_q",
    )(z, *tables)


def _lam_value(lamp_ref, lam_init):
    lp = lamp_ref[...]
    a = jnp.sum(lp[0:1] * lp[1:2], axis=-1, keepdims=True)
    b = jnp.sum(lp[2:3] * lp[3:4], axis=-1, keepdims=True)
    return jnp.exp(a) - jnp.exp(b) + lam_init


def _softmax_pv(qc, chunks):
    m = acc = None
    for kk, vv in chunks:
        kk, vv = kk(), vv()
        s = lax.dot_general(qc, kk, _NT, preferred_element_type=F32)
        cm = jnp.max(s, axis=-1, keepdims=True)
        m_new = cm if m is None else jnp.maximum(m, cm)
        pv = jnp.dot(jnp.exp2(s - m_new).astype(BF16), vv, preferred_element_type=F32)
        acc = pv if m is None else jnp.exp2(m - m_new) * acc + pv
        m = m_new
    return acc[:, :HEAD_W] / acc[:, HEAD_W:]


def _q_halves(q):
    lane = lax.broadcasted_iota(jnp.int32, q.shape, 1)
    zero = jnp.zeros_like(q)
    return jnp.where(lane < HEAD_W // 2, q, zero), jnp.where(lane >= HEAD_W // 2, q, zero)


def _diff_norm(o1, o2, lam, g, lam_init):
    o = o1 - lam * o2
    o = o * lax.rsqrt(jnp.mean(o * o, axis=-1, keepdims=True) + LN_EPS) * g
    return o * (1.0 - lam_init)


def _diff_out(q, chunks, lam, g, lam_init):
    q1, q2 = _q_halves(q)
    return _diff_norm(_softmax_pv(q1, chunks), _softmax_pv(q2, chunks), lam, g, lam_init)


def _half_sq_norms(x):
    xf = x.astype(F32)
    sq = xf * xf
    lane = lax.broadcasted_iota(jnp.int32, sq.shape, 1)
    lo = jnp.sum(jnp.where(lane < HEAD_W // 2, sq, 0.0), axis=-1, keepdims=True)
    return lo, jnp.sum(sq, axis=-1, keepdims=True) - lo


def _half_sq_norm_max(x):
    xf = x.astype(F32)
    half = HEAD_W // 2
    same = (lax.broadcasted_iota(jnp.int32, (HEAD_W, HEAD_W), 0) // half
            == lax.broadcasted_iota(jnp.int32, (HEAD_W, HEAD_W), 1) // half)
    n2 = jnp.dot((xf * xf).astype(BF16), jnp.where(same, 1.0, 0.0).astype(BF16), preferred_element_type=F32)
    return jnp.max(n2, axis=0, keepdims=True)


def _softmax_pv_bounded(qc, bound, key_chunks, p_ref, vall_ref):
    col = 0
    for kk in key_chunks:
        kk = kk()
        s = lax.dot_general(qc, kk, _NT, preferred_element_type=F32)
        p_ref[:, col:col + kk.shape[0]] = jnp.exp2(s - bound).astype(BF16)
        col += kk.shape[0]
    acc = jnp.dot(p_ref[...], vall_ref[...], preferred_element_type=F32)
    return acc[:, :HEAD_W], acc[:, HEAD_W:]


def _attn_prompt_kernel(lamp_ref, q_ref, k_ref, v_ref, g_ref, ks_in, vs_in, o_ref, ok_ref, ov_ref, *, lam_init, hp):
    del ks_in, vs_in
    lam = _lam_value(lamp_ref, lam_init)
    g = g_ref[...]
    for j in range(hp):
        cols = slice(j * HEAD_W, (j + 1) * HEAD_W)
        k = k_ref[:, cols]
        v = v_ref[:, cols]
        ok_ref[j] = k
        ov_ref[j] = v
        q = (q_ref[:, cols] * Q_SCALE).astype(BF16)
        v1 = jnp.concatenate([v.astype(BF16), jnp.ones(v.shape, BF16)], axis=1)
        kb = k.astype(BF16)
        o = _diff_out(q, [(lambda kb=kb: kb, lambda v1=v1: v1)], lam, g, lam_init)
        o_ref[:, cols] = o.astype(BF16)


def _attn_prompt(z, lamp, subln_g, ks_all, vs_all, layer, lam_init, batch, seq, h_diff):
    t_p = batch * seq
    hp = _tile(h_diff, 4, 1)
    w = hp * HEAD_W
    ng = h_diff // hp
    blk = lambda c0: pl.BlockSpec((seq, w), lambda b, g: (b, c0 * ng + g))
    own = pl.BlockSpec((None, None, hp, seq, HEAD_W), lambda b, g: (b, layer, g, 0, 0))
    anywhere = pl.BlockSpec(memory_space=pl.ANY)
    return pl.pallas_call(
        functools.partial(_attn_prompt_kernel, lam_init=lam_init, hp=hp),
        grid=(batch, ng),
        in_specs=[pl.BlockSpec((None, 4, HEAD_W // 2), lambda b, g: (layer, 0, 0)),
                  blk(0), blk(1), blk(2),
                  pl.BlockSpec((None, 1, HEAD_W), lambda b, g: (layer, 0, 0)),
                  anywhere, anywhere],
        out_specs=[pl.BlockSpec((seq, w), lambda b, g: (b, g)), own, own],
        out_shape=[jax.ShapeDtypeStruct((t_p, h_diff * HEAD_W), BF16),
                   jax.ShapeDtypeStruct(ks_all.shape, F32), jax.ShapeDtypeStruct(vs_all.shape, F32)],
        input_output_aliases={5: 1, 6: 2},
        compiler_params=_params(("parallel", "parallel")),
        name="diff_attn_prompt",
    )(lamp, z, z, z, subln_g, ks_all, vs_all)


_MIN_ROW_SUM = 2.0 ** -64


def _attn_sample_kernel(lamp_ref, q_ref, k_ref, v_ref, ck_ref, cv_ref, g_ref, cos_ref, sa_ref, sb_ref,
                        o_ref, kmax_ref, kr_ref, vall_ref, p1_ref, p2_ref, *, lam_init, kc):
    n = k_ref.shape[0]
    past = ck_ref.shape[0]

    @pl.when(pl.program_id(2) == 0)
    def _():
        kr_ref[...] = _rope(k_ref[...], cos_ref[...], sa_ref[...], sb_ref[...]).astype(BF16)
        vall_ref[:past, :] = cv_ref[...]
        vall_ref[past:, :HEAD_W] = v_ref[...].astype(BF16)
        vall_ref[past:, HEAD_W:] = jnp.ones((n, HEAD_W), BF16)
        n2 = jnp.maximum(_half_sq_norm_max(ck_ref[...]), _half_sq_norm_max(kr_ref[...]))
        kmax_ref[0] = jnp.broadcast_to(jnp.sqrt(n2[:, 0:1]) * 1.01, kmax_ref.shape[1:])
        kmax_ref[1] = jnp.broadcast_to(jnp.sqrt(n2[:, HEAD_W - 1:HEAD_W]) * 1.01, kmax_ref.shape[1:])

    keys = [lambda: ck_ref[...]]
    vals = [lambda: vall_ref[:past, :]]
    for c in range(n // kc):
        keys.append(lambda c=c: kr_ref[c * kc:(c + 1) * kc, :])
        vals.append(lambda c=c: vall_ref[past + c * kc:past + (c + 1) * kc, :])
    lam = _lam_value(lamp_ref, lam_init)
    g = g_ref[...]
    q = q_ref[...]

    q1, q2 = _q_halves(q)
    qlo, qhi = _half_sq_norms(q)
    a1, l1 = _softmax_pv_bounded(q1, jnp.sqrt(qlo) * kmax_ref[0, 0:1, 0:1], keys, p1_ref, vall_ref)
    a2, l2 = _softmax_pv_bounded(q2, jnp.sqrt(qhi) * kmax_ref[1, 0:1, 0:1], keys, p2_ref, vall_ref)
    o_ref[...] = _diff_norm(a1 / l1, a2 / l2, lam, g, lam_init).astype(BF16)

    @pl.when(jnp.logical_not(jnp.min(jnp.minimum(l1, l2)) >= _MIN_ROW_SUM))
    def _():
        o_ref[...] = _diff_out(q, list(zip(keys, vals)), lam, g, lam_init).astype(BF16)


def _attn_sample(qb, z, row0, tables, ck, cv, lamp, subln_g, layer, lam_init, dec_batch, dec_seq, h_diff):
    assert row0 % dec_seq == 0
    tq = _tile(dec_seq, 1024)
    kc = _tile(dec_seq, 512)
    nq = dec_seq // tq
    s0 = row0 // dec_seq
    past = ck.shape[3]
    ctx = lambda w: pl.BlockSpec((None, None, None, past, w), lambda b, h, i: (b, layer, h, 0, 0))
    own = lambda g: pl.BlockSpec((dec_seq, HEAD_W), lambda b, h, i: (s0 + b, g * h_diff + h))
    tab_k = pl.BlockSpec((dec_seq, HEAD_W), lambda b, h, i: (0, 0))
    return pl.pallas_call(
        functools.partial(_attn_sample_kernel, lam_init=lam_init, kc=kc),
        grid=(dec_batch, h_diff, nq),
        in_specs=[pl.BlockSpec((None, 4, HEAD_W // 2), lambda b, h, i: (layer, 0, 0)),
                  pl.BlockSpec((tq, HEAD_W), lambda b, h, i: (b * nq + i, h)),
                  own(1), own(2), ctx(HEAD_W), ctx(2 * HEAD_W),
                  pl.BlockSpec((None, 1, HEAD_W), lambda b, h, i: (layer, 0, 0)),
                  tab_k, tab_k, tab_k],
        out_specs=pl.BlockSpec((tq, HEAD_W), lambda b, h, i: (b * nq + i, h)),
        out_shape=jax.ShapeDtypeStruct((dec_batch * dec_seq, h_diff * HEAD_W), BF16),
        scratch_shapes=[pltpu.VMEM((2, 8, HEAD_W), F32), pltpu.VMEM((dec_seq, HEAD_W), BF16),
                        pltpu.VMEM((past + dec_seq, 2 * HEAD_W), BF16),
                        pltpu.VMEM((tq, past + dec_seq), BF16), pltpu.VMEM((tq, past + dec_seq), BF16)],
        compiler_params=_params(("parallel", "parallel", "arbitrary")),
        name="diff_attn_sample",
    )(lamp, qb, z, z, ck, cv, subln_g, *tables)


def _ret_kernel(*refs, has_state, emit_state, nc, scale, hp):
    q_ref, k_ref, v_ref, g_ref, if_ref, ib_ref, cols_ref = refs[:7]
    refs = refs[7:]
    if has_state:
        s0_ref, refs = refs[0], refs[1:]
    if emit_state:
        refs = refs[1:]
    o_ref, refs = refs[0], refs[1:]
    if emit_state:
        s_ref, refs = refs[0], refs[1:]
    of_ref, ob_ref, sf_ref, sb_ref = refs
    c_ = RET_CHUNK
    unroll = _tile(nc, 4, 1)

    def chunk_rows(c):
        return pl.ds(pl.multiple_of(c * c_, c_), c_)

    for j in range(hp):
        lanes = slice(j * HEAD_W, (j + 1) * HEAD_W)
        intra_f = if_ref[j]
        intra_b = ib_ref[j]
        cols = cols_ref[j]
        qdf, kdf, qdb, kdb = cols[:, 0:1], cols[:, 1:2], cols[:, 2:3], cols[:, 3:4]
        cdf, cdb = cols[0:1, 4:5], cols[0:1, 5:6]

        def intra_and_state(c, s, intra, kd, cd, out_ref, st_ref):
            rows = chunk_rows(c)
            st_ref[c] = s.astype(BF16)
            q = q_ref[rows, lanes].astype(BF16)
            k = k_ref[rows, lanes] * scale
            v = v_ref[rows, lanes].astype(BF16)
            a = lax.dot_general(q, k.astype(BF16), _NT, preferred_element_type=F32) * intra
            out_ref[rows, :] = jnp.dot(a.astype(BF16), v, preferred_element_type=F32)
            kt = jnp.transpose(k * kd).astype(BF16)
            return cd * s + jnp.dot(kt, v, preferred_element_type=F32)

        def scan(i, carry):
            sf, sb = carry
            sf = intra_and_state(i, sf, intra_f, kdf, cdf, of_ref, sf_ref)
            sb = intra_and_state(nc - 1 - i, sb, intra_b, kdb, cdb, ob_ref, sb_ref)
            return sf, sb

        if has_state:
            init = (s0_ref[0, j], s0_ref[1, j])
        else:
            init = (jnp.zeros((HEAD_W, HEAD_W), F32), jnp.zeros((HEAD_W, HEAD_W), F32))
        sf, sb = lax.fori_loop(0, nc, scan, init, unroll=unroll)
        if emit_state:
            s_ref[0, j] = sf
            s_ref[1, j] = sb

        def finish(c, carry):
            rows = chunk_rows(c)
            q = q_ref[rows, lanes]
            o = of_ref[rows, :] + ob_ref[rows, :]
            o = o + jnp.dot((q * qdf).astype(BF16), sf_ref[c], preferred_element_type=F32)
            o = o + jnp.dot((q * qdb).astype(BF16), sb_ref[c], preferred_element_type=F32)
            mu = jnp.mean(o, axis=-1, keepdims=True)
            oc = o - mu
            o = oc * lax.rsqrt(jnp.mean(oc * oc, axis=-1, keepdims=True) + LN_EPS)
            g = g_ref[rows, lanes]
            o_ref[rows, lanes] = (o * (g * jax.nn.sigmoid(g))).astype(BF16)
            return carry

        lax.fori_loop(0, nc, finish, 0, unroll=unroll)


def _retention(z, tabs, state, layer, n_seq, seq, row0, h_diff, h_ret, states_all=None):
    emit_state = states_all is not None
    intra_f, intra_b, cols = tabs
    c_ = RET_CHUNK
    nc = seq // c_
    assert row0 % seq == 0
    off = row0 // seq
    hp = _tile(h_ret, 4, 1) if nc <= 4 else 1
    w = hp * HEAD_W
    ng = h_ret // hp
    assert (3 * h_diff) % hp == 0
    c0 = 3 * h_diff // hp
    blk = lambda q: pl.BlockSpec((seq, w), lambda b, g: (off + b, c0 + q * ng + g))
    htab = lambda shape: pl.BlockSpec((hp,) + shape, lambda b, g: (g,) + (0,) * len(shape))
    in_specs = [blk(0), blk(1), blk(2), blk(3), htab((c_, c_)), htab((c_, c_)), htab((c_, 8))]
    args = [z, z, z, z, intra_f, intra_b, cols]
    if state is not None:
        in_specs.append(pl.BlockSpec((None, None, 2, hp, HEAD_W, HEAD_W), lambda b, g: (b, layer, 0, g, 0, 0)))
        args.append(state)
    out_specs = [pl.BlockSpec((seq, w), lambda b, g: (b, g))]
    out_shape = [jax.ShapeDtypeStruct((n_seq * seq, h_ret * HEAD_W), BF16)]
    aliases = {}
    if emit_state:
        aliases = {len(args): 1}
        in_specs.append(pl.BlockSpec(memory_space=pl.ANY))
        args.append(states_all)
        out_specs.append(pl.BlockSpec((None, None, 2, hp, HEAD_W, HEAD_W), lambda b, g: (b, layer, 0, g, 0, 0)))
        out_shape.append(jax.ShapeDtypeStruct(states_all.shape, F32))

    return pl.pallas_call(
        functools.partial(_ret_kernel, has_state=state is not None, emit_state=emit_state, nc=nc,
                          scale=HEAD_W ** -0.5, hp=hp),
        grid=(n_seq, ng),
        in_specs=in_specs,
        out_specs=out_specs,
        out_shape=out_shape,
        scratch_shapes=[pltpu.VMEM((seq, HEAD_W), F32), pltpu.VMEM((seq, HEAD_W), F32),
                        pltpu.VMEM((nc, HEAD_W, HEAD_W), BF16), pltpu.VMEM((nc, HEAD_W, HEAD_W), BF16)],
        input_output_aliases=aliases,
        compiler_params=_params(("parallel", "parallel")),
        name="retention_%d" % seq,
    )(*args)


def _layer_norm(v, g, b):
    mu = jnp.mean(v, axis=-1, keepdims=True)
    vc = v - mu
    var = jnp.mean(vc * vc, axis=-1, keepdims=True)
    return vc * lax.rsqrt(var + LN_EPS) * g + b


def _route(p, n_exp):
    per = n_exp // N_GROUPS
    rows = [p[e:e + 1, :] for e in range(n_exp)]
    scores = []
    for g in range(N_GROUPS):
        a, b, c, d = rows[per * g:per * g + per]
        hi1, lo1, hi2, lo2 = jnp.maximum(a, b), jnp.minimum(a, b), jnp.maximum(c, d), jnp.minimum(c, d)
        scores.append(jnp.maximum(hi1, hi2) + jnp.maximum(jnp.minimum(hi1, hi2), jnp.maximum(lo1, lo2)))
    best = scores[0]
    gsel = jnp.zeros(best.shape, jnp.int32)
    for g in range(1, N_GROUPS):
        better = scores[g] > best
        best = jnp.where(better, scores[g], best)
        gsel = jnp.where(better, g, gsel)
    vals = []
    for j in range(per):
        v = rows[(N_GROUPS - 1) * per + j]
        for g in range(N_GROUPS - 2, -1, -1):
            v = jnp.where(gsel == g, rows[g * per + j], v)
        vals.append(v)
    b1, i1 = vals[0], jnp.zeros(best.shape, jnp.int32)
    for j in range(1, per):
        gt = vals[j] > b1
        b1 = jnp.where(gt, vals[j], b1)
        i1 = jnp.where(gt, j, i1)
    b2, i2 = jnp.full(best.shape, -1.0, F32), jnp.zeros(best.shape, jnp.int32)
    for j in range(per):
        gt = jnp.where(i1 == j, -2.0, vals[j]) > b2
        b2 = jnp.where(gt, vals[j], b2)
        i2 = jnp.where(gt, j, i2)
    tot = b1 + b2
    return gsel * per + i1, gsel * per + i2, b1 / tot, b2 / tot


def _outproj_kernel(odp_ref, ods_ref, orp_ref, ors_ref, wd_ref, wr_ref, x_ref, ga_ref, shf_ref, scf_ref,
                    lng_ref, lnb_ref, wrt_ref, brt_ref, x1_ref, h2_ref, idx_ref, gate_ref, *, alpha, n_p):
    prompt = pl.program_id(0) < n_p
    o_d = jnp.where(prompt, odp_ref[...], ods_ref[...])
    o_r = jnp.where(prompt, orp_ref[...], ors_ref[...])
    mix = jnp.dot(o_d, wd_ref[...], preferred_element_type=F32)
    mix = mix + jnp.dot(o_r, wr_ref[...], preferred_element_type=F32)
    x1 = _layer_norm(alpha * x_ref[...] + ga_ref[...] * mix, lng_ref[...], lnb_ref[...])
    x1_ref[...] = x1
    h2 = x1 * (1.0 + scf_ref[...]) + shf_ref[...]
    h2_ref[...] = h2
    h_hi = h2.astype(BF16)
    h_lo = (h2 - h_hi.astype(F32)).astype(BF16)
    w_hi, w_lo = wrt_ref[0], wrt_ref[1]
    logits = (lax.dot_general(w_hi, h_hi, _NT, preferred_element_type=F32)
              + lax.dot_general(w_lo, h_hi, _NT, preferred_element_type=F32)
              + lax.dot_general(w_hi, h_lo, _NT, preferred_element_type=F32)) + brt_ref[...]
    e = jnp.exp(logits - jnp.max(logits, axis=0, keepdims=True))
    p = e / jnp.sum(e, axis=0, keepdims=True)
    e1, e2, g1, g2 = _route(p, logits.shape[0])
    idx_ref[0:1, :] = e1
    idx_ref[1:2, :] = e2
    gate_ref[0:1, :] = g1
    gate_ref[1:2, :] = g2


def _out_proj(od_p, od_s, or_p, or_s, w_out_b, x, mod6, ln_g, ln_b, w_router_t, b_router_t, layer, tok, alpha):
    t, d = x.shape
    wd = od_p.shape[1]
    wr = or_p.shape[1]
    n_exp = w_router_t.shape[1]
    tm = tok.tile(512)
    row = lambda w: pl.BlockSpec((tm, w), lambda i: (i, 0))
    lnv = pl.BlockSpec((None, None, 1, d), lambda i: (layer, 0, 0, 0))
    return pl.pallas_call(
        functools.partial(_outproj_kernel, alpha=alpha, n_p=tok.t_p // tm),
        grid=(t // tm,),
        in_specs=[tok.group_spec(tm, wd, True), tok.group_spec(tm, wd, False),
                  tok.group_spec(tm, wr, True), tok.group_spec(tm, wr, False),
                  pl.BlockSpec((None, wd, d), lambda i: (layer, 0, 0)),
                  pl.BlockSpec((None, wr, d), lambda i: (layer, wd // wr, 0)),
                  row(d),
                  tok.mod_spec(layer, 2, tm, d), tok.mod_spec(layer, 3, tm, d), tok.mod_spec(layer, 4, tm, d),
                  lnv, lnv,
                  pl.BlockSpec((2, n_exp, d), lambda i: (0, 0, 0)),
                  pl.BlockSpec((n_exp, 1), lambda i: (0, 0))],
        out_specs=[row(d), row(d),
                   pl.BlockSpec((TOP_K, tm), lambda i: (0, i)),
                   pl.BlockSpec((TOP_K, tm), lambda i: (0, i))],
        out_shape=[jax.ShapeDtypeStruct((t, d), F32), jax.ShapeDtypeStruct((t, d), F32),
                   jax.ShapeDtypeStruct((TOP_K, t), jnp.int32), jax.ShapeDtypeStruct((TOP_K, t), F32)],
        compiler_params=_params(("parallel",)),
        name="out_proj_ln_router",
    )(od_p, od_s, or_p, or_s, w_out_b, w_out_b, x, mod6, mod6, mod6, ln_g, ln_b, w_router_t, b_router_t)


def _moe_kernel(be_ref, cnt_ref, tok0_ref, tokn_ref, dst_ref, gate_ref, x_hbm, w1_ref, w3_ref, w2_ref, y_hbm,
                xbuf, ybuf, gsem, ssem, *, n_out):
    del be_ref
    s = pl.program_id(0)
    rows = xbuf.shape[1]
    c_old, c_cur, c_next = cnt_ref[s], cnt_ref[s + 2], cnt_ref[s + 3]

    def gather_copy(tok_ref, r, p):
        return pltpu.make_async_copy(x_hbm.at[pl.ds(tok_ref[0, 0, r], 1), :], xbuf.at[p, pl.ds(r, 1), :], gsem.at[p])

    def scatter_copy(r, p):
        return pltpu.make_async_copy(ybuf.at[p, pl.ds(r, 1), :], y_hbm.at[pl.ds(dst_ref[0, 0, r], 1), :], ssem.at[p])

    @pl.when(s == 0)
    def _():
        for p in range(2):
            ybuf[p] = jnp.zeros(ybuf.shape[1:], F32)
            spare = pltpu.make_async_copy(ybuf.at[p], y_hbm.at[pl.ds(n_out + p * rows, rows), :], ssem.at[p])
            spare.start()
            spare.wait()

        def first(r, c):
            gather_copy(tok0_ref, r, 0).start()
            return c

        lax.fori_loop(0, rows, first, 0)

    for p in range(2):
        mine = s % 2 == p

        @pl.when(jnp.logical_and(mine, c_old > 0))
        def _():
            pltpu.make_async_copy(ybuf.at[p], y_hbm.at[pl.ds(0, rows), :], ssem.at[p]).wait()

        @pl.when(jnp.logical_and(mine, c_next > 0))
        def _():
            for r in range(rows):
                gather_copy(tokn_ref, r, 1 - p).start()

        @pl.when(jnp.logical_and(mine, c_cur > 0))
        def _():
            pltpu.make_async_copy(x_hbm.at[pl.ds(0, rows), :], xbuf.at[p], gsem.at[p]).wait()
            x = xbuf[p].astype(BF16)
            a = jnp.dot(x, w1_ref[...], preferred_element_type=F32)
            g = jnp.dot(x, w3_ref[...], preferred_element_type=F32)
            h = (a * jax.nn.sigmoid(a) * g).astype(BF16)
            ybuf[p] = jnp.dot(h, w2_ref[...], preferred_element_type=F32) * gate_ref[...]
            for r in range(rows):
                scatter_copy(r, p).start()


def _experts(h2, block_e, block_cnt, slot_tok, slot_dst, slot_gate, w1b, w3b, w2b, layer, n_out):
    t, d = h2.shape
    nb = block_e.shape[0]
    ff = w1b.shape[-1]
    rows = MOE_ROWS
    last = nb - 1
    cnt_pad = jnp.concatenate([jnp.zeros((2,), jnp.int32), block_cnt, jnp.zeros((3,), jnp.int32)])
    smem = lambda off: pl.BlockSpec((1, 1, rows), lambda s, be, n: (jnp.minimum(s + off, last), 0, 0),
                                    memory_space=pltpu.SMEM)
    wspec = lambda shape: pl.BlockSpec((None, None) + shape,
                                       lambda s, be, n: (layer, be[jnp.minimum(s, last)], 0, 0))
    grid_spec = pltpu.PrefetchScalarGridSpec(
        num_scalar_prefetch=2,
        grid=(nb + 2,),
        in_specs=[smem(0), smem(1), smem(0),
                  pl.BlockSpec((rows, 1), lambda s, be, n: (jnp.minimum(s, last), 0)),
                  pl.BlockSpec(memory_space=pl.ANY),
                  wspec((d, ff)), wspec((d, ff)), wspec((ff, d))],
        out_specs=pl.BlockSpec(memory_space=pl.ANY),
        scratch_shapes=[pltpu.VMEM((2, rows, d), F32), pltpu.VMEM((2, rows, d), F32),
                        pltpu.SemaphoreType.DMA((2,)), pltpu.SemaphoreType.DMA((2,))])
    tok3 = slot_tok.reshape(nb, 1, rows)
    return pl.pallas_call(
        functools.partial(_moe_kernel, n_out=n_out),
        grid_spec=grid_spec,
        out_shape=jax.ShapeDtypeStruct((n_out + 2 * rows, d), F32),
        compiler_params=_params(("arbitrary",)),
        name="moe_experts",
    )(block_e, cnt_pad, tok3, tok3, slot_dst.reshape(nb, 1, rows), slot_gate.reshape(nb * rows, 1),
      h2, w1b, w3b, w2b)


def _dispatch_plan(idx, gate, n_exp, t):
    rows = MOE_ROWS
    a = t * TOP_K
    flat_e = idx.T.reshape(a)
    flat_g = gate.T.reshape(a)
    _, order = lax.sort((flat_e, jnp.arange(a, dtype=jnp.int32)), num_keys=1, is_stable=True)
    experts = jnp.arange(n_exp, dtype=jnp.int32)
    counts = jnp.sum((flat_e[None, :] == experts[:, None]).astype(jnp.int32), axis=1)
    padded = (counts + rows - 1) // rows * rows
    ends_p = jnp.cumsum(padded)
    starts_p = ends_p - padded
    starts = jnp.cumsum(counts) - counts
    nb = -(-a // rows) + n_exp
    blk = jnp.arange(nb, dtype=jnp.int32)
    block_e = jnp.minimum(jnp.sum((blk[:, None] * rows >= ends_p[None, :]).astype(jnp.int32), axis=1), n_exp - 1)
    onehot = block_e[:, None] == experts[None, :]
    pick = lambda tab: jnp.sum(jnp.where(onehot, tab[None, :], 0), axis=1)
    within = (blk * rows - pick(starts_p))[:, None] + jnp.arange(rows, dtype=jnp.int32)[None, :]
    valid = within < pick(counts)[:, None]
    src = order[jnp.clip(pick(starts)[:, None] + within, 0, a - 1)]
    slot_tok = jnp.where(valid, src // TOP_K, 0)
    spare = a + (blk % 2)[:, None] * rows + jnp.arange(rows, dtype=jnp.int32)[None, :]
    slot_dst = jnp.where(valid, (src % TOP_K) * t + src // TOP_K, spare)
    slot_gate = jnp.where(valid, flat_g[src], 0.0)
    block_cnt = jnp.sum(valid.astype(jnp.int32), axis=1)
    return block_e, block_cnt, slot_tok, slot_dst, slot_gate


def _ln2_kernel(*refs, alpha, emit_h, n_p):
    x1_ref, y0_ref, y1_ref, gf_ref, lng_ref, lnb_ref = refs[:6]
    y = y0_ref[...] + y1_ref[...]
    x2 = _layer_norm(alpha * x1_ref[...] + gf_ref[...] * y, lng_ref[...], lnb_ref[...])
    if emit_h:
        sh_ref, sc_ref, x2_ref, h_ref = refs[6:]
        h_ref[...] = (x2 * (1.0 + sc_ref[...]) + sh_ref[...]).astype(BF16)
        x2_ref[...] = x2
    else:
        xp_ref, xs_ref = refs[6:]

        @pl.when(pl.program_id(0) < n_p)
        def _():
            xp_ref[...] = x2

        @pl.when(pl.program_id(0) >= n_p)
        def _():
            xs_ref[...] = x2


def _ln2(x1, y2, mod6, ln_g, ln_b, layer, tok, alpha, last):
    t, d = x1.shape
    tm = tok.tile(512)
    nt = t // tm
    row = lambda off: pl.BlockSpec((tm, d), lambda i: (off + i, 0))
    lnv = pl.BlockSpec((None, None, 1, d), lambda i: (layer, 1, 0, 0))
    in_specs = [row(0), row(0), row(nt), tok.mod_spec(layer, 5, tm, d), lnv, lnv]
    args = [x1, y2, y2, mod6, ln_g, ln_b]
    if last:
        out_specs = [tok.group_spec(tm, d, True), tok.group_spec(tm, d, False)]
        out_shape = [jax.ShapeDtypeStruct((tok.t_p, d), F32), jax.ShapeDtypeStruct((tok.t_s, d), F32)]
    else:
        in_specs += [tok.mod_spec(layer + 1, 0, tm, d), tok.mod_spec(layer + 1, 1, tm, d)]
        args += [mod6, mod6]
        out_specs = [row(0), row(0)]
        out_shape = [jax.ShapeDtypeStruct((t, d), F32), jax.ShapeDtypeStruct((t, d), BF16)]
    return pl.pallas_call(
        functools.partial(_ln2_kernel, alpha=alpha, emit_h=not last, n_p=tok.t_p // tm),
        grid=(nt,),
        in_specs=in_specs,
        out_specs=out_specs,
        out_shape=out_shape,
        compiler_params=_params(("arbitrary",)),
        name="ln2_mod",
    )(*args)


def _rope_tables(n_tok):
    rows = n_tok // GRID_W
    axis = HEAD_W // 4
    r = jnp.repeat(jnp.arange(rows, dtype=F32), GRID_W)
    col = jnp.tile(jnp.arange(GRID_W, dtype=F32), rows)
    inv = ROPE_BASE ** (-jnp.arange(0, axis, 2, dtype=F32) / axis)
    ar = r[:, None] * inv[None]
    ac = col[:, None] * inv[None]
    ang = jnp.concatenate([ar, ar, ac, ac], -1)
    cos, sin = jnp.cos(ang), jnp.sin(ang)
    cos, sin = jnp.tile(cos, (1, 2)), jnp.tile(sin, (1, 2))
    low = (jnp.arange(HEAD_W) % (axis)) < axis // 2
    return cos, jnp.where(low, -sin, 0.0), jnp.where(low, 0.0, sin)


def _ret_tables(h_ret):
    c_ = RET_CHUNK
    h = jnp.arange(h_ret, dtype=F32)
    lg_f = jnp.log(1.0 - jnp.exp2(-5.0 - h))
    lg_b = jnp.log(1.0 - jnp.exp2(-5.5 - h))
    pos = jnp.arange(c_, dtype=F32)
    dist = pos[:, None] - pos[None, :]
    intra_f = jnp.where(dist >= 0, jnp.exp(lg_f[:, None, None] * jnp.maximum(dist, 0.0)), 0.0)
    intra_b = jnp.where(dist <= 0, jnp.exp(lg_b[:, None, None] * jnp.maximum(-dist, 0.0)), 0.0)
    qd_f = jnp.exp(lg_f[:, None] * (pos + 1.0))
    kd_f = jnp.exp(lg_f[:, None] * (c_ - 1.0 - pos))
    qd_b = jnp.exp(lg_b[:, None] * (c_ - pos))
    kd_b = jnp.exp(lg_b[:, None] * pos)
    cd_f = jnp.broadcast_to(jnp.exp(lg_f * c_)[:, None], (h_ret, c_))
    cd_b = jnp.broadcast_to(jnp.exp(lg_b * c_)[:, None], (h_ret, c_))
    zero = jnp.zeros((h_ret, c_), F32)
    cols = jnp.stack([qd_f, kd_f, qd_b, kd_b, cd_f, cd_b, zero, zero], axis=2)
    return intra_f, intra_b, cols


def kernel(x_prompt, x_sample, c, cache_k, cache_v, state_ret, c_ctx, w_mod, b_mod, w_in, w_out,
           lam_q1, lam_k1, lam_q2, lam_k2, subln_g, ln_g, ln_b, w_router, b_router, w1, w3, w2):
    batch, seq, d = x_prompt.shape
    dec_batch, dec_seq, _ = x_sample.shape
    depth = w_mod.shape[0]
    h_diff = cache_k.shape[2]
    h_ret = state_ret.shape[3]
    n_exp = w1.shape[1]
    assert cache_k.shape[-1] == cache_v.shape[-1] == HEAD_W
    assert state_ret.shape[-2:] == (HEAD_W, HEAD_W)
    alpha = (2 * depth) ** 0.25

    n_cond = 8
    tok = _Tokens(batch, seq, dec_batch, dec_seq, n_cond)
    cs = jnp.concatenate([c_ctx[None], c, jnp.zeros((n_cond - 1 - dec_batch, d), F32)], 0)
    mod6 = _modulation(cs, w_mod, b_mod).reshape(depth * n_cond * 6, 1, d)

    w_in_b = w_in.astype(BF16)
    w_out_b = w_out.astype(BF16)
    w1b, w3b, w2b = w1.astype(BF16), w3.astype(BF16), w2.astype(BF16)
    ck_b = cache_k.astype(BF16)
    cv_b = jnp.concatenate([cache_v.astype(BF16), jnp.ones(cache_v.shape, BF16)], axis=-1)
    lamp = jnp.stack([lam_q1, lam_k1, lam_q2, lam_k2], axis=1)
    subln = subln_g.reshape(depth, 1, HEAD_W)
    ln_g4 = ln_g.reshape(depth, 2, 1, d)
    ln_b4 = ln_b.reshape(depth, 2, 1, d)
    w_router_hi = w_router.T.astype(BF16)
    w_router_t = jnp.stack([w_router_hi, (w_router.T - w_router_hi.astype(F32)).astype(BF16)])
    b_router_t = b_router.reshape(n_exp, 1)
    rope = _rope_tables(dec_seq)
    ret_tabs = _ret_tables(h_ret)

    x, h = _premod(x_prompt.reshape(tok.t_p, d), x_sample.reshape(tok.t_s, d), mod6, tok)
    seq_heads = (batch, depth, h_diff, seq, HEAD_W)
    ks_all, vs_all = jnp.zeros(seq_heads, F32), jnp.zeros(seq_heads, F32)
    ss_all = jnp.zeros((batch, depth, 2, h_ret, HEAD_W, HEAD_W), F32)
    for layer in range(depth):
        lam_init = 0.8 - 0.6 * math.exp(-0.3 * layer)
        z = _in_proj(h, w_in_b, layer)
        od_p, ks_all, vs_all = _attn_prompt(z, lamp, subln, ks_all, vs_all, layer, lam_init, batch, seq, h_diff)
        qb = _rope_q(z, rope, tok, h_diff)
        od_s = _attn_sample(qb, z, tok.t_p, rope, ck_b, cv_b, lamp, subln, layer, lam_init, dec_batch, dec_seq,
                            h_diff)
        or_p, ss_all = _retention(z, ret_tabs, None, layer, batch, seq, 0, h_diff, h_ret, ss_all)
        (or_s,) = _retention(z, ret_tabs, state_ret, layer, dec_batch, dec_seq, tok.t_p, h_diff, h_ret)
        x1, h2, idx, gate = _out_proj(od_p, od_s, or_p, or_s, w_out_b, x, mod6, ln_g4, ln_b4,
                                      w_router_t, b_router_t, layer, tok, alpha)
        plan = _dispatch_plan(idx, gate, n_exp, tok.t)
        y2 = _experts(h2, *plan, w1b, w3b, w2b, layer, TOP_K * tok.t)
        last = layer == depth - 1
        outs = _ln2(x1, y2, mod6, ln_g4, ln_b4, layer, tok, alpha, last)
        x, h = outs
    xp = x.reshape(batch, seq, d)
    xs = h.reshape(dec_batch, dec_seq, d)
    return (xp, xs, ks_all, vs_all, ss_all)
```
